```python
import jax, jax.numpy as jnp
from jax import lax
import numpy as np

D_MODEL = 2048
BATCH = 2
SEQ = 4096
DEPTH = 4
DEC_BATCH = 8
DEC_SEQ = 1
PAST_LEN = 16384
PAGE_SIZE = 128

N_MIXERS = 2
N_ATTN_LAYERS = (DEPTH + 1) // 2
N_POOL_LAYERS = DEPTH // 2
N_HEADS = 16
HEAD_DIM = D_MODEL // N_HEADS
MOBA_BLOCK = 256
MOBA_TOPK = 3
QUERY_CHUNK = 32
ROPE_THETA = 10000.0
POOL_WINDOWS = (2, 4, 8, 16)
N_POOL_GROUPS = len(POOL_WINDOWS)
POOL_GROUP_DIM = D_MODEL // N_POOL_GROUPS
POOL_STATE_LEN = max(POOL_WINDOWS) - 1
D_FF = -(-8 * D_MODEL // (3 * 256)) * 256
EPS = 1e-6

kernel_name = "moba_pool_hybrid_decode_step"

F32 = jnp.float32


def rms_norm(x, g):
    xf = x.astype(F32)
    y = xf * lax.rsqrt(jnp.mean(xf * xf, axis=-1, keepdims=True) + EPS)
    return (y * g.astype(F32)).astype(x.dtype)


def rope(x, pos):
    half = HEAD_DIM // 2
    inv = ROPE_THETA ** (-jnp.arange(half, dtype=F32) / half)
    ang = pos.astype(F32)[:, None] * inv[None, :]
    cos = jnp.cos(ang)[None, :, None, :]
    sin = jnp.sin(ang)[None, :, None, :]
    xf = x.astype(F32)
    x1, x2 = xf[..., :half], xf[..., half:]
    return jnp.concatenate([x1 * cos - x2 * sin, x2 * cos + x1 * sin], axis=-1).astype(x.dtype)


def attn_qkv(xn, pos, w_qkv, q_g, k_g):
    B, L, _ = xn.shape
    qkv = (xn @ w_qkv).reshape(B, L, 3, N_HEADS, HEAD_DIM)
    q = rope(rms_norm(qkv[:, :, 0], q_g), pos)
    k = rope(rms_norm(qkv[:, :, 1], k_g), pos)
    v = qkv[:, :, 2]
    return q, k, v


def moba_attention(q, k, v, q_start):
    B, Lq, H, hd = q.shape
    Lk = k.shape[1]
    nb = -(-Lk // MOBA_BLOCK)
    kpad = nb * MOBA_BLOCK - Lk
    kp = jnp.pad(k, ((0, 0), (0, kpad), (0, 0), (0, 0)))
    vp = jnp.pad(v, ((0, 0), (0, kpad), (0, 0), (0, 0)))
    kb = kp.reshape(B, nb, MOBA_BLOCK, H, hd)
    vb = vp.reshape(B, nb, MOBA_BLOCK, H, hd)
    k_mean = jnp.mean(kb.astype(F32), axis=2).astype(q.dtype)
    n_sel = min(MOBA_TOPK, nb)
    qc = QUERY_CHUNK if Lq >= QUERY_CHUNK else Lq
    n_chunks = -(-Lq // qc)
    qpad = n_chunks * qc - Lq
    qq = jnp.pad(q, ((0, 0), (0, qpad), (0, 0), (0, 0)))
    qq = qq.reshape(B, n_chunks, qc, H, hd).transpose(1, 0, 2, 3, 4)
    scale = HEAD_DIM ** -0.5
    b_idx = jnp.arange(B)[:, None, None, None]
    h_idx = jnp.arange(H)[None, None, :, None]
    blk_ids = jnp.arange(nb)

    def chunk(args):
        c, q_c = args
        start = q_start + c * qc
        own = start // MOBA_BLOCK
        q_pos = start + jnp.arange(qc)
        gate = jnp.einsum('bqhd,bnhd->bqhn', q_c, k_mean).astype(F32)
        gate = jnp.where(blk_ids < own, gate, -jnp.inf)
        _, sel = lax.top_k(gate, n_sel)
        sel_valid = jnp.arange(n_sel) < own
        gk = kb[b_idx, sel, :, h_idx]
        gv = vb[b_idx, sel, :, h_idx]
        s_sel = jnp.einsum('bqhd,bqhstd->bqhst', q_c, gk).astype(F32) * scale
        s_sel = jnp.where(sel_valid[:, None], s_sel, -jnp.inf).reshape(B, qc, H, n_sel * MOBA_BLOCK)
        ok = lax.dynamic_slice_in_dim(kp, own * MOBA_BLOCK, MOBA_BLOCK, axis=1)
        ov = lax.dynamic_slice_in_dim(vp, own * MOBA_BLOCK, MOBA_BLOCK, axis=1)
        s_own = jnp.einsum('bqhd,bthd->bqht', q_c, ok).astype(F32) * scale
        k_pos = own * MOBA_BLOCK + jnp.arange(MOBA_BLOCK)
        causal = (k_pos[None, :] <= q_pos[:, None])[None, :, None, :]
        s_own = jnp.where(causal, s_own, -jnp.inf)
        p = jax.nn.softmax(jnp.concatenate([s_sel, s_own], axis=-1), axis=-1).astype(v.dtype)
        p_sel = p[..., :n_sel * MOBA_BLOCK].reshape(B, qc, H, n_sel, MOBA_BLOCK)
        p_own = p[..., n_sel * MOBA_BLOCK:]
        return (jnp.einsum('bqhst,bqhstd->bqhd', p_sel, gv)
                + jnp.einsum('bqht,bthd->bqhd', p_own, ov))

    out = lax.map(chunk, (jnp.arange(n_chunks), qq))
    out = out.transpose(1, 0, 2, 3, 4).reshape(B, n_chunks * qc, H, hd)
    return out[:, :Lq]


def pool_mix(xn, prefix, start_pos, w_pool, scale):
    B, L, D = xn.shape
    full = jnp.concatenate([prefix.astype(xn.dtype), xn], axis=1)
    fullf = full.astype(F32)
    cs = jnp.concatenate([jnp.zeros((B, 1, D), F32), jnp.cumsum(fullf, axis=1)], axis=1)
    end = POOL_STATE_LEN + 1
    t = jnp.arange(L)
    outs = []
    for g, w in enumerate(POOL_WINDOWS):
        lo, hi = g * POOL_GROUP_DIM, (g + 1) * POOL_GROUP_DIM
        win = cs[:, end:end + L, lo:hi] - cs[:, end - w:end - w + L, lo:hi]
        cnt = jnp.minimum(start_pos + t + 1, w).astype(F32)
        outs.append(win / cnt[None, :, None])
    pooled = jnp.concatenate(outs, axis=-1)
    d = (pooled - xn.astype(F32)).astype(xn.dtype).reshape(B, L, N_POOL_GROUPS, POOL_GROUP_DIM)
    y = jnp.einsum('blgc,gcd->blgd', d, w_pool).reshape(B, L, D)
    return y * scale, full[:, -POOL_STATE_LEN:]


def swiglu(xn, w_gu, w_dn):
    g, u = jnp.split(xn @ w_gu, 2, axis=-1)
    return (jax.nn.silu(g) * u) @ w_dn


def setup_inputs(seed: int = 0) -> dict:
    key = jax.random.key(seed)
    ks = jax.random.split(key, 18)
    n_pages = PAST_LEN // PAGE_SIZE
    n_used = DEC_BATCH * n_pages
    n_phys = n_used + (n_used + 3) // 4
    D = D_MODEL
    nrm = jax.random.normal
    page_table = jax.random.permutation(ks[0], n_phys)[:n_used].reshape(DEC_BATCH, n_pages).astype(jnp.int32)
    return {
        "x_prompt": nrm(ks[1], (BATCH, SEQ, D), F32),
        "x_sample": nrm(ks[2], (DEC_BATCH, DEC_SEQ, D), F32),
        "cache_k": nrm(ks[3], (N_ATTN_LAYERS, n_phys, PAGE_SIZE, N_HEADS, HEAD_DIM), F32),
        "cache_v": nrm(ks[4], (N_ATTN_LAYERS, n_phys, PAGE_SIZE, N_HEADS, HEAD_DIM), F32),
        "state_pool": nrm(ks[5], (N_POOL_LAYERS, DEC_BATCH, POOL_STATE_LEN, D), F32),
        "page_table": page_table,
        "norm_mix": 1.0 + 0.05 * nrm(ks[6], (DEPTH, D), F32),
        "norm_ffn": 1.0 + 0.05 * nrm(ks[7], (DEPTH, D), F32),
        "w_qkv": nrm(ks[8], (N_ATTN_LAYERS, D, 3 * D), F32) * D ** -0.5,
        "w_o": nrm(ks[9], (N_ATTN_LAYERS, D, D), F32) * D ** -0.5,
        "q_norm": 1.0 + 0.05 * nrm(ks[10], (N_ATTN_LAYERS, HEAD_DIM), F32),
        "k_norm": 1.0 + 0.05 * nrm(ks[11], (N_ATTN_LAYERS, HEAD_DIM), F32),
        "w_pool": nrm(ks[12], (N_POOL_LAYERS, N_POOL_GROUPS, POOL_GROUP_DIM, POOL_GROUP_DIM), F32) * POOL_GROUP_DIM ** -0.5,
        "pool_scale": 1.0 + 0.1 * nrm(ks[13], (N_POOL_LAYERS, D), F32),
        "w_gate_up": nrm(ks[14], (DEPTH, D, 2 * D_FF), F32) * D ** -0.5,
        "w_down": nrm(ks[15], (DEPTH, D_FF, D), F32) * D_FF ** -0.5,
    }


def reference(x_prompt, x_sample, cache_k, cache_v, state_pool, page_table, norm_mix, norm_ffn,
              w_qkv, w_o, q_norm, k_norm, w_pool, pool_scale, w_gate_up, w_down):
    Bp, S, D = x_prompt.shape
    Bd, L, _ = x_sample.shape
    n_pages = page_table.shape[1]
    past = n_pages * PAGE_SIZE
    pos_p = jnp.arange(S)
    pos_s = past + jnp.arange(L)
    kp_list, vp_list, ks_list, vs_list, sp_list, ss_list = [], [], [], [], [], []
    xp, xs = x_prompt, x_sample
    for i in range(DEPTH):
        hp = rms_norm(xp, norm_mix[i])
        hs = rms_norm(xs, norm_mix[i])
        if i % N_MIXERS == 0:
            a = i // N_MIXERS
            qp, kp, vp = attn_qkv(hp, pos_p, w_qkv[a], q_norm[a], k_norm[a])
            op = moba_attention(qp, kp, vp, 0)
            qs, kn, vn = attn_qkv(hs, pos_s, w_qkv[a], q_norm[a], k_norm[a])
            past_k = cache_k[a, page_table].reshape(Bd, past, N_HEADS, HEAD_DIM).astype(kn.dtype)
            past_v = cache_v[a, page_table].reshape(Bd, past, N_HEADS, HEAD_DIM).astype(vn.dtype)
            os_ = moba_attention(qs, jnp.concatenate([past_k, kn], axis=1),
                                 jnp.concatenate([past_v, vn], axis=1), past)
            xp = xp + op.reshape(Bp, S, D) @ w_o[a]
            xs = xs + os_.reshape(Bd, L, D) @ w_o[a]
            kp_list.append(kp); vp_list.append(vp); ks_list.append(kn); vs_list.append(vn)
        else:
            p = i // N_MIXERS
            zero_prefix = jnp.zeros((Bp, POOL_STATE_LEN, D), hp.dtype)
            yp, stp = pool_mix(hp, zero_prefix, 0, w_pool[p], pool_scale[p])
            ys, sts = pool_mix(hs, state_pool[p], past, w_pool[p], pool_scale[p])
            xp = xp + yp
            xs = xs + ys
            sp_list.append(stp); ss_list.append(sts)
        xp = xp + swiglu(rms_norm(xp, norm_ffn[i]), w_gate_up[i], w_down[i])
        xs = xs + swiglu(rms_norm(xs, norm_ffn[i]), w_gate_up[i], w_down[i])
    return (xp, xs, jnp.stack(kp_list), jnp.stack(vp_list), jnp.stack(ks_list), jnp.stack(vs_list),
            jnp.stack(sp_list), jnp.stack(ss_list))
```

```python
import functools

import jax
import jax.numpy as jnp
import numpy as np
from jax import lax
from jax.experimental import pallas as pl
from jax.experimental.pallas import tpu as pltpu

F32 = jnp.float32
BF16 = jnp.bfloat16

N_HEADS = 16
MOBA_BLOCK = 256
MOBA_TOPK = 3
ROPE_THETA = 10000.0
POOL_WINDOWS = (2, 4, 8, 16)
POOL_STATE_LEN = max(POOL_WINDOWS) - 1
POOL_CARRY_ROWS = 16
EPS = 1e-6
NEG_INF = float("-inf")

V7X_VMEM_BYTES = 64 * 1024 * 1024
V7X_LANES = 128
TEMPORARIES_VMEM_BYTES = 12 * 1024 * 1024


def _nbytes(shape, dtype):
    return int(np.prod(shape)) * jnp.dtype(dtype).itemsize


def _vmem_limit(pipelined, scratch=()):
    total = sum(2 * _nbytes(s, d) for s, d in pipelined) + sum(_nbytes(s, d) for s, d in scratch)
    return min(total + TEMPORARIES_VMEM_BYTES, V7X_VMEM_BYTES - 4 * 1024 * 1024)


def _params(semantics, pipelined, scratch=()):
    return pltpu.CompilerParams(dimension_semantics=semantics, vmem_limit_bytes=_vmem_limit(pipelined, scratch))


def _rms(x, g):
    return x * lax.rsqrt(jnp.mean(x * x, axis=-1, keepdims=True) + EPS) * g


def _dot(a, b):
    return jnp.dot(a, b, preferred_element_type=F32)


def _qkv_kernel(x_ref, g_ref, wq_ref, wk_ref, wv_ref, qg_ref, kg_ref, cos_ref, sin_ref,
                q_ref, k_ref, v_ref, xn_ref, *, head_dim):
    @pl.when(pl.program_id(1) == 0)
    def _():
        xn_ref[...] = _rms(x_ref[...], g_ref[...]).astype(BF16)

    xn = xn_ref[...]
    cos = cos_ref[...]
    sin = sin_ref[...]
    heads_per_tile = q_ref.shape[1] // head_dim

    def norm_rope(y, gain, out_ref):
        for h in range(heads_per_tile):
            yh = y[:, h * head_dim:(h + 1) * head_dim]
            yh = _rms(yh, gain)
            rot = pltpu.roll(yh, head_dim // 2, axis=1)
            out_ref[:, h * head_dim:(h + 1) * head_dim] = yh * cos + rot * sin

    norm_rope(_dot(xn, wq_ref[...]), qg_ref[...], q_ref)
    norm_rope(_dot(xn, wk_ref[...]), kg_ref[...], k_ref)
    v_ref[...] = _dot(xn, wv_ref[...])


def _qkv(x, g, w_qkv, qg, kg, cos, sin, *, tm, tn):
    M, D = x.shape
    hd = D // N_HEADS
    nn = D // tn
    row = lambda m, n: (m, 0)
    col = lambda off: (lambda m, n: (0, n + off * nn))
    out = pl.BlockSpec((tm, tn), lambda m, n: (m, n))
    blocks = [((tm, D), F32), ((D, tn), BF16), ((D, tn), BF16), ((D, tn), BF16),
              ((tm, hd), F32), ((tm, hd), F32), ((tm, tn), F32), ((tm, tn), F32), ((tm, tn), F32)]
    return pl.pallas_call(
        functools.partial(_qkv_kernel, head_dim=hd),
        grid=(M // tm, nn),
        in_specs=[pl.BlockSpec((tm, D), row), pl.BlockSpec((1, D), lambda m, n: (0, 0)),
                  pl.BlockSpec((D, tn), col(0)), pl.BlockSpec((D, tn), col(1)), pl.BlockSpec((D, tn), col(2)),
                  pl.BlockSpec((1, hd), lambda m, n: (0, 0)), pl.BlockSpec((1, hd), lambda m, n: (0, 0)),
                  pl.BlockSpec((tm, hd), row), pl.BlockSpec((tm, hd), row)],
        out_specs=[out, out, out],
        out_shape=[jax.ShapeDtypeStruct((M, D), F32)] * 3,
        scratch_shapes=[pltpu.VMEM((tm, D), BF16)],
        compiler_params=_params(("parallel", "arbitrary"), blocks, [((tm, D), BF16)]),
        name="qkv_proj",
    )(x, g, w_qkv, w_qkv, w_qkv, qg, kg, cos, sin)


def _topk_rank(g, n_valid_blocks, axis_iota):
    nb = g.shape[0]
    rank = jnp.zeros(g.shape, jnp.int32)
    for m in range(nb):
        gm = g[m:m + 1, :]
        beats = (gm > g) | ((gm == g) & (m < axis_iota))
        rank = rank + beats.astype(jnp.int32)
    return rank


def _moba_prompt_kernel(q_ref, k_ref, v_ref, o_ref, kb_ref, vt_ref, km_ref, bias_ref, *, scale):
    blk = MOBA_BLOCK
    seq, hd = k_ref.shape
    nb = seq // blk
    j = pl.program_id(2)

    @pl.when(j == 0)
    def _():
        kb_ref[...] = k_ref[...].astype(BF16)
        for n in range(nb):
            km_ref[n:n + 1, :] = jnp.mean(k_ref[n * blk:(n + 1) * blk, :], axis=0, keepdims=True)
            vt_ref[n] = v_ref[n * blk:(n + 1) * blk, :].T.astype(BF16)

    q = q_ref[...]
    gate = lax.dot_general(km_ref[...], q, (((1,), (1,)), ((), ())),
                           precision=lax.Precision.HIGHEST, preferred_element_type=F32)
    blk_id = lax.broadcasted_iota(jnp.int32, gate.shape, 0)
    past = blk_id < j
    gate = jnp.where(past, gate, NEG_INF)
    rank = _topk_rank(gate, j, blk_id)
    bias_ref[...] = jnp.where((rank < MOBA_TOPK) & past, 0.0, NEG_INF)

    qt = (q * scale).T.astype(BF16)

    own = pl.multiple_of(j * blk, blk)
    s = _dot(kb_ref[pl.ds(own, blk), :], qt)
    kpos = lax.broadcasted_iota(jnp.int32, s.shape, 0)
    qpos = lax.broadcasted_iota(jnp.int32, s.shape, 1)
    s = jnp.where(kpos <= qpos, s, NEG_INF)
    m0 = jnp.max(s, axis=0, keepdims=True)
    p = jnp.exp(s - m0)
    l0 = jnp.sum(p, axis=0, keepdims=True)
    acc0 = _dot(vt_ref[j], p.astype(BF16))

    def body(n, carry):
        m, l, acc = carry
        start = pl.multiple_of(n * blk, blk)
        s = _dot(kb_ref[pl.ds(start, blk), :], qt) + bias_ref[pl.ds(n, 1), :]
        m_new = jnp.maximum(m, jnp.max(s, axis=0, keepdims=True))
        alpha = jnp.exp(m - m_new)
        p = jnp.exp(s - m_new)
        l = alpha * l + jnp.sum(p, axis=0, keepdims=True)
        acc = alpha * acc + _dot(vt_ref[n], p.astype(BF16))
        return m_new, l, acc

    _, l, acc = lax.fori_loop(0, j, body, (m0, l0, acc0))
    o_ref[...] = (acc / l).T.astype(o_ref.dtype)


def _moba_prompt(q, k, v):
    B, S, D = q.shape
    hd = D // N_HEADS
    assert hd == V7X_LANES and S % MOBA_BLOCK == 0
    nb = S // MOBA_BLOCK
    tile = pl.BlockSpec((None, MOBA_BLOCK, hd), lambda b, h, j: (b, j, h))
    full = pl.BlockSpec((None, S, hd), lambda b, h, j: (b, 0, h))
    scratch = [((S, hd), BF16), ((nb, hd, MOBA_BLOCK), BF16), ((nb, hd), F32), ((nb, MOBA_BLOCK), F32)]
    return pl.pallas_call(
        functools.partial(_moba_prompt_kernel, scale=hd ** -0.5),
        grid=(B, N_HEADS, nb),
        in_specs=[tile, full, full],
        out_specs=tile,
        out_shape=jax.ShapeDtypeStruct((B, S, D), BF16),
        scratch_shapes=[pltpu.VMEM(s, d) for s, d in scratch],
        compiler_params=_params(("parallel", "parallel", "arbitrary"),
                                [((MOBA_BLOCK, hd), F32), ((S, hd), F32), ((S, hd), F32), ((MOBA_BLOCK, hd), BF16)],
                                scratch),
        name="moba_prompt",
    )(q, k, v)


def _oproj_kernel(o_ref, w_ref, x_ref, y_ref):
    y_ref[...] = x_ref[...] + _dot(o_ref[...], w_ref[...])


def _oproj(o, w, x, *, tm):
    M, D = x.shape
    row = pl.BlockSpec((tm, D), lambda m: (m, 0))
    return pl.pallas_call(
        _oproj_kernel,
        grid=(M // tm,),
        in_specs=[row, pl.BlockSpec((D, D), lambda m: (0, 0)), row],
        out_specs=row,
        out_shape=jax.ShapeDtypeStruct((M, D), F32),
        compiler_params=_params(("parallel",), [((tm, D), BF16), ((D, D), BF16), ((tm, D), F32), ((tm, D), F32)]),
        name="out_proj",
    )(o, w, x)


def _ffn_kernel(x_ref, g_ref, wg_ref, wu_ref, wd_ref, y_ref, xn_ref):
    @pl.when(pl.program_id(1) == 0)
    def _():
        x = x_ref[...]
        xn_ref[...] = _rms(x, g_ref[...]).astype(BF16)
        y_ref[...] = x

    xn = xn_ref[...]
    gate = _dot(xn, wg_ref[...])
    up = _dot(xn, wu_ref[...])
    h = (gate / (1.0 + jnp.exp(-gate)) * up).astype(BF16)
    y_ref[...] += _dot(h, wd_ref[...])


def _ffn(x, g, w_gu, w_dn, *, tm, tf):
    M, D = x.shape
    d_ff = w_dn.shape[0]
    nf = d_ff // tf
    assert nf * tf == d_ff
    row = pl.BlockSpec((tm, D), lambda m, f: (m, 0))
    blocks = [((tm, D), F32), ((D, tf), BF16), ((D, tf), BF16), ((tf, D), BF16), ((tm, D), F32)]
    return pl.pallas_call(
        _ffn_kernel,
        grid=(M // tm, nf),
        in_specs=[row, pl.BlockSpec((1, D), lambda m, f: (0, 0)),
                  pl.BlockSpec((D, tf), lambda m, f: (0, f)), pl.BlockSpec((D, tf), lambda m, f: (0, f + nf)),
                  pl.BlockSpec((tf, D), lambda m, f: (f, 0))],
        out_specs=row,
        out_shape=jax.ShapeDtypeStruct((M, D), F32),
        scratch_shapes=[pltpu.VMEM((tm, D), BF16)],
        compiler_params=_params(("parallel", "arbitrary"), blocks, [((tm, D), BF16)]),
        name="swiglu_ffn",
    )(x, g, w_gu, w_gu, w_dn)


def _pool_prompt_kernel(x_ref, g_ref, w_ref, sc_ref, y_ref, tail_ref, ext_ref):
    tm, D = x_ref.shape
    carry = POOL_CARRY_ROWS
    gd = D // len(POOL_WINDOWS)
    si = pl.program_id(1)

    @pl.when(si == 0)
    def _():
        ext_ref[0:carry, :] = jnp.zeros((carry, D), F32)

    x = x_ref[...]
    h = _rms(x, g_ref[...])
    ext_ref[carry:carry + tm, :] = h
    pos = si * tm + lax.broadcasted_iota(jnp.int32, (tm, 1), 0)
    for gi, w in enumerate(POOL_WINDOWS):
        lo, hi = gi * gd, (gi + 1) * gd
        hg = h[:, lo:hi]
        win = hg
        for back in range(1, w):
            win = win + ext_ref[carry - back:carry - back + tm, lo:hi]
        cnt = jnp.minimum(pos + 1, w).astype(F32)
        d = (win / cnt - hg).astype(BF16)
        y_ref[:, lo:hi] = x[:, lo:hi] + _dot(d, w_ref[gi]) * sc_ref[:, lo:hi]
    last = h[tm - carry:tm, :]
    ext_ref[0:carry, :] = last
    tail_ref[...] = last


def _pool_prompt(x, g, w_pool, sc, *, tm):
    B, S, D = x.shape
    gd = D // len(POOL_WINDOWS)
    assert S % tm == 0 and tm >= POOL_CARRY_ROWS
    row = pl.BlockSpec((None, tm, D), lambda b, s: (b, s, 0))
    vec = pl.BlockSpec((1, D), lambda b, s: (0, 0))
    return pl.pallas_call(
        _pool_prompt_kernel,
        grid=(B, S // tm),
        in_specs=[row, vec, pl.BlockSpec((len(POOL_WINDOWS), gd, gd), lambda b, s: (0, 0, 0)), vec],
        out_specs=[row, pl.BlockSpec((None, POOL_CARRY_ROWS, D), lambda b, s: (b, 0, 0))],
        out_shape=[jax.ShapeDtypeStruct((B, S, D), F32), jax.ShapeDtypeStruct((B, POOL_CARRY_ROWS, D), F32)],
        scratch_shapes=[pltpu.VMEM((POOL_CARRY_ROWS + tm, D), F32)],
        compiler_params=_params(("parallel", "arbitrary"),
                                [((tm, D), F32), ((len(POOL_WINDOWS), gd, gd), BF16), ((tm, D), F32)],
                                [((POOL_CARRY_ROWS + tm, D), F32)]),
        name="pool_prompt",
    )(x, g, w_pool, sc)


def _pool_sample_kernel(x_ref, pre_ref, g_ref, w_ref, sc_ref, y_ref, h_ref, *, counts):
    D = x_ref.shape[1]
    gd = D // len(POOL_WINDOWS)
    x = x_ref[...]
    h = _rms(x, g_ref[...])
    h_ref[...] = h
    n_pre = pre_ref.shape[1]
    for gi, w in enumerate(POOL_WINDOWS):
        lo, hi = gi * gd, (gi + 1) * gd
        hg = h[:, lo:hi]
        win = hg
        for back in range(1, w):
            win = win + pre_ref[:, n_pre - back, lo:hi]
        d = (win / counts[gi] - hg).astype(BF16)
        y_ref[:, lo:hi] = x[:, lo:hi] + _dot(d, w_ref[gi]) * sc_ref[:, lo:hi]


def _pool_sample(x, prefix, g, w_pool, sc, *, start_pos):
    B, D = x.shape
    counts = tuple(float(min(start_pos + 1, w)) for w in POOL_WINDOWS)
    return pl.pallas_call(
        functools.partial(_pool_sample_kernel, counts=counts),
        out_shape=[jax.ShapeDtypeStruct((B, D), F32), jax.ShapeDtypeStruct((B, D), F32)],
        name="pool_sample",
    )(x, prefix, g, w_pool, sc)


def _block_ksum_kernel(pt_ref, *refs):
    page_refs, o_ref = refs[:-1], refs[-1]
    total = jnp.sum(page_refs[0][...], axis=0)
    for r in page_refs[1:]:
        total = total + jnp.sum(r[...], axis=0)
    o_ref[...] = total


def _block_ksum(cache_k, page_table, layer, *, pages_per_block):
    _, _, page, H, hd = cache_k.shape
    B, n_pages = page_table.shape
    n_blocks = n_pages // pages_per_block

    def page_spec(i):
        return pl.BlockSpec((None, None, page, H, hd),
                            lambda b, n, pt: (layer, pt[b, n * pages_per_block + i], 0, 0, 0))

    return pl.pallas_call(
        _block_ksum_kernel,
        grid_spec=pltpu.PrefetchScalarGridSpec(
            num_scalar_prefetch=1,
            grid=(B, n_blocks),
            in_specs=[page_spec(i) for i in range(pages_per_block)],
            out_specs=pl.BlockSpec((None, None, H, hd), lambda b, n, pt: (b, n, 0, 0)),
        ),
        out_shape=jax.ShapeDtypeStruct((B, n_blocks, H, hd), F32),
        compiler_params=_params(("parallel", "parallel"),
                                [((page, H, hd), F32)] * pages_per_block + [((H, hd), F32)]),
        name="block_ksum",
    )(page_table, *([cache_k] * pages_per_block))


def _sample_select_kernel(ks_ref, q_ref, sel_ref):
    k_mean = ks_ref[...] * (1.0 / MOBA_BLOCK)
    gate = jnp.sum(k_mean * q_ref[...][None, :, :], axis=-1)
    blk_id = lax.broadcasted_iota(jnp.int32, gate.shape, 0)
    n_blocks = gate.shape[0]
    rank = _topk_rank(gate, n_blocks, blk_id)
    slot = lax.broadcasted_iota(jnp.int32, sel_ref.shape, 0)
    sel = jnp.zeros(sel_ref.shape, jnp.int32)
    for r in range(MOBA_TOPK):
        picked = jnp.sum(jnp.where(rank == r, blk_id, 0), axis=0, keepdims=True)
        sel = jnp.where(slot == r, picked, sel)
    sel_ref[...] = sel


def _sample_select(ksum, q):
    B, n_blocks, H, hd = ksum.shape
    assert n_blocks >= MOBA_TOPK
    return pl.pallas_call(
        _sample_select_kernel,
        grid=(B,),
        in_specs=[pl.BlockSpec((None, n_blocks, H, hd), lambda b: (b, 0, 0, 0)),
                  pl.BlockSpec((None, H, hd), lambda b: (b, 0, 0))],
        out_specs=pl.BlockSpec((None, 8, H), lambda b: (b, 0, 0)),
        out_shape=jax.ShapeDtypeStruct((B, 8, H), jnp.int32),
        compiler_params=_params(("parallel",), [((n_blocks, H, hd), F32), ((H, hd), F32), ((8, H), jnp.int32)]),
        name="sample_select",
    )(ksum, q)


HEAD_GROUP = 8


def _sample_attn_kernel(pages_ref, q_ref, kn_ref, vn_ref, kp_ref, vp_ref, o_ref, m_ref, l_ref, acc_ref, *, scale):
    h_in_group = pl.program_id(1) % HEAD_GROUP
    t = pl.program_id(2)
    q = q_ref[...] * scale

    @pl.when(t == 0)
    def _():
        m_ref[...] = jnp.sum(q * kn_ref[...], axis=-1, keepdims=True)
        l_ref[...] = jnp.ones(l_ref.shape, F32)
        acc_ref[...] = vn_ref[...]

    s = jnp.sum(kp_ref[...] * q[None, :, :], axis=-1, keepdims=True)
    m_old = m_ref[...]
    m_new = jnp.maximum(m_old, jnp.max(s, axis=0))
    alpha = jnp.exp(m_old - m_new)
    p = jnp.exp(s - m_new[None, :, :])
    l_ref[...] = alpha * l_ref[...] + jnp.sum(p, axis=0)
    acc_ref[...] = alpha * acc_ref[...] + jnp.sum(p * vp_ref[...], axis=0)
    m_ref[...] = m_new

    @pl.when(t == pl.num_programs(2) - 1)
    def _():
        row = pl.ds(h_in_group, 1)
        o_ref[row, :] = acc_ref[row, :] / l_ref[row, :]


def _sample_attn(pages, q, k_new, v_new, cache_k, cache_v, layer, *, n_sel_pages):
    B, H, hd = q.shape
    page = cache_k.shape[2]
    assert H % HEAD_GROUP == 0
    vec = pl.BlockSpec((None, HEAD_GROUP, hd), lambda b, h, t, pg: (b, h // HEAD_GROUP, 0))
    tile = pl.BlockSpec((None, None, page, HEAD_GROUP, hd),
                        lambda b, h, t, pg: (layer, pg[(b * H + h) * n_sel_pages + t], 0, h // HEAD_GROUP, 0))
    return pl.pallas_call(
        functools.partial(_sample_attn_kernel, scale=hd ** -0.5),
        grid_spec=pltpu.PrefetchScalarGridSpec(
            num_scalar_prefetch=1,
            grid=(B, H, n_sel_pages),
            in_specs=[vec, vec, vec, tile, tile],
            out_specs=vec,
            scratch_shapes=[pltpu.VMEM((HEAD_GROUP, 1), F32), pltpu.VMEM((HEAD_GROUP, 1), F32),
                            pltpu.VMEM((HEAD_GROUP, hd), F32)],
        ),
        out_shape=jax.ShapeDtypeStruct((B, H, hd), F32),
        compiler_params=_params(("parallel", "arbitrary", "arbitrary"),
                                [((HEAD_GROUP, hd), F32)] * 4 + [((page, HEAD_GROUP, hd), F32)] * 2),
        name="sample_attn",
    )(pages, q, k_new, v_new, cache_k, cache_v)


def _rope_tables(pos, head_dim):
    half = head_dim // 2
    inv = ROPE_THETA ** (-jnp.arange(half, dtype=F32) / half)
    ang = pos.astype(F32)[:, None] * inv[None, :]
    cos, sin = jnp.cos(ang), jnp.sin(ang)
    return jnp.concatenate([cos, cos], axis=1), jnp.concatenate([-sin, sin], axis=1)


def kernel(x_prompt, x_sample, cache_k, cache_v, state_pool, page_table, norm_mix, norm_ffn, w_qkv, w_o, q_norm, k_norm, w_pool, pool_scale, w_gate_up, w_down):
    Bp, S, D = x_prompt.shape
    Bd, L, _ = x_sample.shape
    depth = norm_mix.shape[0]
    n_layers_attn, n_phys, page_size, n_heads, hd = cache_k.shape
    n_pages = page_table.shape[1]
    past = n_pages * page_size
    assert n_heads == N_HEADS and L == 1 and MOBA_BLOCK % page_size == 0 and past % MOBA_BLOCK == 0
    pages_per_block = MOBA_BLOCK // page_size
    n_sel_pages = MOBA_TOPK * pages_per_block

    w_qkv_b, w_o_b = w_qkv.astype(BF16), w_o.astype(BF16)
    w_gu_b, w_dn_b, w_pool_b = w_gate_up.astype(BF16), w_down.astype(BF16), w_pool.astype(BF16)
    cos_p, sin_p = _rope_tables(jnp.tile(jnp.arange(S), Bp), hd)
    cos_s, sin_s = _rope_tables(jnp.tile(past + jnp.arange(L), Bd), hd)

    Mp, Ms = Bp * S, Bd * L
    xp = x_prompt.reshape(Mp, D)
    xs = x_sample.reshape(Ms, D)
    tm_p, tm_s = 512, Ms
    kp_list, vp_list, ks_list, vs_list, sp_list, ss_list = [], [], [], [], [], []
    for i in range(depth):
        g_mix = norm_mix[i][None, :]
        g_ffn = norm_ffn[i][None, :]
        if i % 2 == 0:
            a = i // 2
            qg, kg = q_norm[a][None, :], k_norm[a][None, :]
            q, k, v = _qkv(xp, g_mix, w_qkv_b[a], qg, kg, cos_p, sin_p, tm=tm_p, tn=512)
            o = _moba_prompt(q.reshape(Bp, S, D), k.reshape(Bp, S, D), v.reshape(Bp, S, D))
            xp = _oproj(o.reshape(Mp, D), w_o_b[a], xp, tm=tm_p)
            kp_list.append(k.reshape(Bp, S, N_HEADS, hd))
            vp_list.append(v.reshape(Bp, S, N_HEADS, hd))
            qs, kn, vn = _qkv(xs, g_mix, w_qkv_b[a], qg, kg, cos_s, sin_s, tm=tm_s, tn=512)
            ksum = _block_ksum(cache_k, page_table, a, pages_per_block=pages_per_block)
            sel = _sample_select(ksum, qs.reshape(Bd, N_HEADS, hd))[:, :MOBA_TOPK, :]
            logical = sel[:, :, :, None] * pages_per_block + jnp.arange(pages_per_block)
            logical = logical.transpose(0, 2, 1, 3).reshape(Bd, N_HEADS * n_sel_pages)
            pages = jnp.take_along_axis(page_table, logical, axis=1).reshape(-1)
            os_ = _sample_attn(pages, qs.reshape(Bd, N_HEADS, hd), kn.reshape(Bd, N_HEADS, hd),
                               vn.reshape(Bd, N_HEADS, hd), cache_k, cache_v, a, n_sel_pages=n_sel_pages)
            xs = _oproj(os_.reshape(Ms, D).astype(BF16), w_o_b[a], xs, tm=tm_s)
            ks_list.append(kn.reshape(Bd, L, N_HEADS, hd))
            vs_list.append(vn.reshape(Bd, L, N_HEADS, hd))
        else:
            p = i // 2
            sc = pool_scale[p][None, :]
            y, tail = _pool_prompt(xp.reshape(Bp, S, D), g_mix, w_pool_b[p], sc, tm=tm_p)
            xp = y.reshape(Mp, D)
            sp_list.append(tail[:, POOL_CARRY_ROWS - POOL_STATE_LEN:, :])
            xs, hs = _pool_sample(xs, state_pool[p], g_mix, w_pool_b[p], sc, start_pos=past)
            ss_list.append(jnp.concatenate([state_pool[p][:, 1:, :], hs[:, None, :]], axis=1))
        xp = _ffn(xp, g_ffn, w_gu_b[i], w_dn_b[i], tm=tm_p, tf=512)
        xs = _ffn(xs, g_ffn, w_gu_b[i], w_dn_b[i], tm=tm_s, tf=512)
    return (xp.reshape(Bp, S, D), xs.reshape(Bd, L, D), jnp.stack(kp_list), jnp.stack(vp_list),
            jnp.stack(ks_list), jnp.stack(vs_list), jnp.stack(sp_list), jnp.stack(ss_list))
```

```python
import functools

import jax
import jax.numpy as jnp
import numpy as np
from jax import lax
from jax.experimental import pallas as pl
from jax.experimental.pallas import tpu as pltpu

F32 = jnp.float32
BF16 = jnp.bfloat16

N_HEADS = 16
MOBA_BLOCK = 256
MOBA_TOPK = 3
ROPE_THETA = 10000.0
POOL_WINDOWS = (2, 4, 8, 16)
POOL_STATE_LEN = max(POOL_WINDOWS) - 1
POOL_CARRY_ROWS = 16
EPS = 1e-6
NEG_INF = float("-inf")

V7X_VMEM_BYTES = 64 * 1024 * 1024
V7X_LANES = 128
TEMPORARIES_VMEM_BYTES = 12 * 1024 * 1024


def _nbytes(shape, dtype):
    return int(np.prod(shape)) * jnp.dtype(dtype).itemsize


def _vmem_limit(pipelined, scratch=()):
    total = sum(2 * _nbytes(s, d) for s, d in pipelined) + sum(_nbytes(s, d) for s, d in scratch)
    return min(total + TEMPORARIES_VMEM_BYTES, V7X_VMEM_BYTES - 4 * 1024 * 1024)


def _params(semantics, pipelined, scratch=()):
    return pltpu.CompilerParams(dimension_semantics=semantics, vmem_limit_bytes=_vmem_limit(pipelined, scratch))


def _rms(x, g):
    return x * lax.rsqrt(jnp.mean(x * x, axis=-1, keepdims=True) + EPS) * g


def _dot(a, b):
    return jnp.dot(a, b, preferred_element_type=F32)


def _qkv_kernel(x_ref, g_ref, wq_ref, wk_ref, wv_ref, qg_ref, kg_ref, cos_ref, sin_ref,
                q_ref, k_ref, v_ref, xn_ref, *, head_dim):
    @pl.when(pl.program_id(1) == 0)
    def _():
        xn_ref[...] = _rms(x_ref[...], g_ref[...]).astype(BF16)

    xn = xn_ref[...]
    cos = cos_ref[...]
    sin = sin_ref[...]
    heads_per_tile = q_ref.shape[1] // head_dim

    def norm_rope(y, gain, out_ref):
        for h in range(heads_per_tile):
            yh = y[:, h * head_dim:(h + 1) * head_dim]
            yh = _rms(yh, gain)
            rot = pltpu.roll(yh, head_dim // 2, axis=1)
            out_ref[:, h * head_dim:(h + 1) * head_dim] = yh * cos + rot * sin

    norm_rope(_dot(xn, wq_ref[...]), qg_ref[...], q_ref)
    norm_rope(_dot(xn, wk_ref[...]), kg_ref[...], k_ref)
    v_ref[...] = _dot(xn, wv_ref[...])


def _qkv(x, g, w_qkv, qg, kg, cos, sin, *, tm, tn):
    M, D = x.shape
    hd = D // N_HEADS
    nn = D // tn
    row = lambda m, n: (m, 0)
    col = lambda off: (lambda m, n: (0, n + off * nn))
    out = pl.BlockSpec((tm, tn), lambda m, n: (m, n))
    blocks = [((tm, D), F32), ((D, tn), BF16), ((D, tn), BF16), ((D, tn), BF16),
              ((tm, hd), F32), ((tm, hd), F32), ((tm, tn), F32), ((tm, tn), F32), ((tm, tn), F32)]
    return pl.pallas_call(
        functools.partial(_qkv_kernel, head_dim=hd),
        grid=(M // tm, nn),
        in_specs=[pl.BlockSpec((tm, D), row), pl.BlockSpec((1, D), lambda m, n: (0, 0)),
                  pl.BlockSpec((D, tn), col(0)), pl.BlockSpec((D, tn), col(1)), pl.BlockSpec((D, tn), col(2)),
                  pl.BlockSpec((1, hd), lambda m, n: (0, 0)), pl.BlockSpec((1, hd), lambda m, n: (0, 0)),
                  pl.BlockSpec((tm, hd), row), pl.BlockSpec((tm, hd), row)],
        out_specs=[out, out, out],
        out_shape=[jax.ShapeDtypeStruct((M, D), F32)] * 3,
        scratch_shapes=[pltpu.VMEM((tm, D), BF16)],
        compiler_params=_params(("parallel", "arbitrary"), blocks, [((tm, D), BF16)]),
        name="qkv_proj",
    )(x, g, w_qkv, w_qkv, w_qkv, qg, kg, cos, sin)


def _topk_rank(g, blk_id):
    rank = jnp.zeros(g.shape, jnp.int32)
    for m in range(g.shape[0]):
        gm = g[m:m + 1]
        beats = (gm > g) | ((gm == g) & (m < blk_id))
        rank = rank + beats.astype(jnp.int32)
    return rank


Q_GROUP = 4


def _moba_prompt_kernel(q_ref, k_ref, v_ref, o_ref, kb_ref, vt_ref, km_ref, bias_ref, s_ref, acc_ref, *, scale):
    blk = MOBA_BLOCK
    seq, hd = k_ref.shape
    nb = seq // blk
    tq = q_ref.shape[0]
    g = pl.program_id(2)
    first = g * Q_GROUP

    @pl.when(g == 0)
    def _():
        kb_ref[...] = k_ref[...].astype(BF16)
        for n in range(nb):
            km_ref[n:n + 1, :] = jnp.mean(k_ref[n * blk:(n + 1) * blk, :], axis=0, keepdims=True)
            vt_ref[n] = v_ref[n * blk:(n + 1) * blk, :].T.astype(BF16)

    q = q_ref[...]
    gate = lax.dot_general(km_ref[...], q, (((1,), (1,)), ((), ())),
                           precision=lax.Precision.HIGHEST, preferred_element_type=F32)
    blk_id = lax.broadcasted_iota(jnp.int32, gate.shape, 0)
    own = first + lax.broadcasted_iota(jnp.int32, gate.shape, 1) // blk
    past = blk_id < own
    rank = _topk_rank(jnp.where(past, gate, NEG_INF), blk_id)
    bias_ref[...] = jnp.where((rank < MOBA_TOPK) & past, 0.0, NEG_INF)

    qt = (q * scale).T.astype(BF16)

    def scores(n):
        return _dot(kb_ref[pl.ds(pl.multiple_of(n * blk, blk), blk), :], qt)

    def sublane_groups(x):
        return x.reshape(blk // 8, 8, tq)

    def past_scores(c, mx):
        for u in range(Q_GROUP):
            n = c * Q_GROUP + u
            s = scores(n) + bias_ref[pl.ds(n, 1), :]
            s_ref[n] = s
            mx = jnp.maximum(mx, jnp.max(sublane_groups(s), axis=0))
        return mx

    mx = lax.fori_loop(0, g, past_scores, jnp.full((8, tq), NEG_INF, F32))

    kpos = lax.broadcasted_iota(jnp.int32, (blk, tq), 0)
    qcol = lax.broadcasted_iota(jnp.int32, (blk, tq), 1)
    for u in range(Q_GROUP):
        n = first + u
        s = scores(n)
        in_own = (qcol >= u * blk) & (qcol < (u + 1) * blk)
        s = jnp.where(in_own, jnp.where(kpos <= qcol - u * blk, s, NEG_INF), s + bias_ref[pl.ds(n, 1), :])
        s_ref[n] = s
        mx = jnp.maximum(mx, jnp.max(sublane_groups(s), axis=0))
    m = jnp.max(mx, axis=0, keepdims=True)

    def weigh(n, l8):
        p = jnp.exp(s_ref[n] - m)
        acc_ref[...] += _dot(vt_ref[n], p.astype(BF16))
        return l8 + jnp.sum(sublane_groups(p), axis=0)

    def past_weigh(c, l8):
        for u in range(Q_GROUP):
            l8 = weigh(c * Q_GROUP + u, l8)
        return l8

    acc_ref[...] = jnp.zeros(acc_ref.shape, F32)
    l8 = lax.fori_loop(0, g, past_weigh, jnp.zeros((8, tq), F32))
    for u in range(Q_GROUP):
        l8 = weigh(first + u, l8)
    l = jnp.sum(l8, axis=0, keepdims=True)
    o_ref[...] = (acc_ref[...] / l).T.astype(o_ref.dtype)


def _moba_prompt(q, k, v):
    B, S, D = q.shape
    hd = D // N_HEADS
    tq = Q_GROUP * MOBA_BLOCK
    assert hd == V7X_LANES and S % tq == 0
    nb = S // MOBA_BLOCK
    tile = pl.BlockSpec((None, tq, hd), lambda b, h, g: (b, g, h))
    full = pl.BlockSpec((None, S, hd), lambda b, h, g: (b, 0, h))
    scratch = [((S, hd), BF16), ((nb, hd, MOBA_BLOCK), BF16), ((nb, hd), F32), ((nb, tq), F32),
               ((nb, MOBA_BLOCK, tq), F32), ((hd, tq), F32)]
    return pl.pallas_call(
        functools.partial(_moba_prompt_kernel, scale=hd ** -0.5),
        grid=(B, N_HEADS, S // tq),
        in_specs=[tile, full, full],
        out_specs=tile,
        out_shape=jax.ShapeDtypeStruct((B, S, D), BF16),
        scratch_shapes=[pltpu.VMEM(s, d) for s, d in scratch],
        compiler_params=_params(("parallel", "parallel", "arbitrary"),
                                [((tq, hd), F32), ((S, hd), F32), ((S, hd), F32), ((tq, hd), BF16)], scratch),
        name="moba_prompt",
    )(q, k, v)


def _oproj_kernel(o_ref, w_ref, x_ref, y_ref):
    y_ref[...] = x_ref[...] + _dot(o_ref[...], w_ref[...])


def _oproj(o, w, x, *, tm):
    M, D = x.shape
    row = pl.BlockSpec((tm, D), lambda m: (m, 0))
    return pl.pallas_call(
        _oproj_kernel,
        grid=(M // tm,),
        in_specs=[row, pl.BlockSpec((D, D), lambda m: (0, 0)), row],
        out_specs=row,
        out_shape=jax.ShapeDtypeStruct((M, D), F32),
        compiler_params=_params(("parallel",), [((tm, D), BF16), ((D, D), BF16), ((tm, D), F32), ((tm, D), F32)]),
        name="out_proj",
    )(o, w, x)


def _ffn_kernel(x_ref, g_ref, wg_ref, wu_ref, wd_ref, y_ref, xn_ref):
    @pl.when(pl.program_id(1) == 0)
    def _():
        x = x_ref[...]
        xn_ref[...] = _rms(x, g_ref[...]).astype(BF16)
        y_ref[...] = x

    xn = xn_ref[...]
    gate = _dot(xn, wg_ref[...])
    up = _dot(xn, wu_ref[...])
    h = (gate / (1.0 + jnp.exp(-gate)) * up).astype(BF16)
    y_ref[...] += _dot(h, wd_ref[...])


def _ffn(x, g, w_gu, w_dn, *, tm, tf):
    M, D = x.shape
    d_ff = w_dn.shape[0]
    nf = d_ff // tf
    assert nf * tf == d_ff
    row = pl.BlockSpec((tm, D), lambda m, f: (m, 0))
    blocks = [((tm, D), F32), ((D, tf), BF16), ((D, tf), BF16), ((tf, D), BF16), ((tm, D), F32)]
    return pl.pallas_call(
        _ffn_kernel,
        grid=(M // tm, nf),
        in_specs=[row, pl.BlockSpec((1, D), lambda m, f: (0, 0)),
                  pl.BlockSpec((D, tf), lambda m, f: (0, f)), pl.BlockSpec((D, tf), lambda m, f: (0, f + nf)),
                  pl.BlockSpec((tf, D), lambda m, f: (f, 0))],
        out_specs=row,
        out_shape=jax.ShapeDtypeStruct((M, D), F32),
        scratch_shapes=[pltpu.VMEM((tm, D), BF16)],
        compiler_params=_params(("parallel", "arbitrary"), blocks, [((tm, D), BF16)]),
        name="swiglu_ffn",
    )(x, g, w_gu, w_gu, w_dn)


def _pool_prompt_kernel(x_ref, g_ref, w_ref, sc_ref, y_ref, tail_ref, ext_ref):
    tm, D = x_ref.shape
    carry = POOL_CARRY_ROWS
    gd = D // len(POOL_WINDOWS)
    si = pl.program_id(1)

    @pl.when(si == 0)
    def _():
        ext_ref[0:carry, :] = jnp.zeros((carry, D), F32)

    x = x_ref[...]
    h = _rms(x, g_ref[...])
    ext_ref[carry:carry + tm, :] = h
    pos = si * tm + lax.broadcasted_iota(jnp.int32, (tm, 1), 0)
    for gi, w in enumerate(POOL_WINDOWS):
        lo, hi = gi * gd, (gi + 1) * gd
        hg = h[:, lo:hi]
        win = hg
        for back in range(1, w):
            win = win + ext_ref[carry - back:carry - back + tm, lo:hi]
        cnt = jnp.minimum(pos + 1, w).astype(F32)
        d = (win / cnt - hg).astype(BF16)
        y_ref[:, lo:hi] = x[:, lo:hi] + _dot(d, w_ref[gi]) * sc_ref[:, lo:hi]
    last = h[tm - carry:tm, :]
    ext_ref[0:carry, :] = last
    tail_ref[...] = last


def _pool_prompt(x, g, w_pool, sc, *, tm):
    B, S, D = x.shape
    gd = D // len(POOL_WINDOWS)
    assert S % tm == 0 and tm >= POOL_CARRY_ROWS
    row = pl.BlockSpec((None, tm, D), lambda b, s: (b, s, 0))
    vec = pl.BlockSpec((1, D), lambda b, s: (0, 0))
    return pl.pallas_call(
        _pool_prompt_kernel,
        grid=(B, S // tm),
        in_specs=[row, vec, pl.BlockSpec((len(POOL_WINDOWS), gd, gd), lambda b, s: (0, 0, 0)), vec],
        out_specs=[row, pl.BlockSpec((None, POOL_CARRY_ROWS, D), lambda b, s: (b, 0, 0))],
        out_shape=[jax.ShapeDtypeStruct((B, S, D), F32), jax.ShapeDtypeStruct((B, POOL_CARRY_ROWS, D), F32)],
        scratch_shapes=[pltpu.VMEM((POOL_CARRY_ROWS + tm, D), F32)],
        compiler_params=_params(("parallel", "arbitrary"),
                                [((tm, D), F32), ((len(POOL_WINDOWS), gd, gd), BF16), ((tm, D), F32)],
                                [((POOL_CARRY_ROWS + tm, D), F32)]),
        name="pool_prompt",
    )(x, g, w_pool, sc)


def _pool_sample_kernel(x_ref, pre_ref, g_ref, w_ref, sc_ref, y_ref, h_ref, *, counts):
    D = x_ref.shape[1]
    gd = D // len(POOL_WINDOWS)
    x = x_ref[...]
    h = _rms(x, g_ref[...])
    h_ref[...] = h
    n_pre = pre_ref.shape[1]
    for gi, w in enumerate(POOL_WINDOWS):
        lo, hi = gi * gd, (gi + 1) * gd
        hg = h[:, lo:hi]
        win = hg
        for back in range(1, w):
            win = win + pre_ref[:, n_pre - back, lo:hi]
        d = (win / counts[gi] - hg).astype(BF16)
        y_ref[:, lo:hi] = x[:, lo:hi] + _dot(d, w_ref[gi]) * sc_ref[:, lo:hi]


def _pool_sample(x, prefix, g, w_pool, sc, *, start_pos):
    B, D = x.shape
    counts = tuple(float(min(start_pos + 1, w)) for w in POOL_WINDOWS)
    return pl.pallas_call(
        functools.partial(_pool_sample_kernel, counts=counts),
        out_shape=[jax.ShapeDtypeStruct((B, D), F32), jax.ShapeDtypeStruct((B, D), F32)],
        name="pool_sample",
    )(x, prefix, g, w_pool, sc)


def _block_ksum_kernel(pt_ref, *refs, pages_per_block):
    page_refs, o_ref = refs[:-1], refs[-1]
    for i in range(o_ref.shape[0]):
        total = jnp.sum(page_refs[i * pages_per_block][...], axis=0)
        for r in page_refs[i * pages_per_block + 1:(i + 1) * pages_per_block]:
            total = total + jnp.sum(r[...], axis=0)
        o_ref[i] = total


KSUM_BLOCKS_PER_STEP = 2


def _block_ksum(cache_k, page_table, layer, *, pages_per_block):
    _, _, page, H, hd = cache_k.shape
    B, n_pages = page_table.shape
    n_blocks = n_pages // pages_per_block
    pages_per_step = KSUM_BLOCKS_PER_STEP * pages_per_block
    assert n_blocks % KSUM_BLOCKS_PER_STEP == 0

    def page_spec(i):
        return pl.BlockSpec((None, None, page, H, hd),
                            lambda b, n, pt: (layer, pt[b, n * pages_per_step + i], 0, 0, 0))

    return pl.pallas_call(
        functools.partial(_block_ksum_kernel, pages_per_block=pages_per_block),
        grid_spec=pltpu.PrefetchScalarGridSpec(
            num_scalar_prefetch=1,
            grid=(B, n_blocks // KSUM_BLOCKS_PER_STEP),
            in_specs=[page_spec(i) for i in range(pages_per_step)],
            out_specs=pl.BlockSpec((None, KSUM_BLOCKS_PER_STEP, H, hd), lambda b, n, pt: (b, n, 0, 0)),
        ),
        out_shape=jax.ShapeDtypeStruct((B, n_blocks, H, hd), F32),
        compiler_params=_params(("parallel", "parallel"),
                                [((page, H, hd), F32)] * pages_per_step + [((KSUM_BLOCKS_PER_STEP, H, hd), F32)]),
        name="block_ksum",
    )(page_table, *([cache_k] * pages_per_step))


def _sample_select_kernel(ks_ref, q_ref, sel_ref):
    k_mean = ks_ref[...] * (1.0 / MOBA_BLOCK)
    gate = jnp.sum(k_mean * q_ref[...][None, :, :], axis=-1, keepdims=True)
    blk_id = lax.broadcasted_iota(jnp.int32, gate.shape, 0)
    rank = _topk_rank(gate, blk_id)
    for r in range(MOBA_TOPK):
        picked = jnp.sum(jnp.where(rank == r, blk_id, 0), axis=0)
        sel_ref[r] = jnp.broadcast_to(picked, sel_ref.shape[1:])


def _sample_select(ksum, q):
    B, n_blocks, H, hd = ksum.shape
    assert n_blocks >= MOBA_TOPK
    sel = pl.pallas_call(
        _sample_select_kernel,
        grid=(B,),
        in_specs=[pl.BlockSpec((None, n_blocks, H, hd), lambda b: (b, 0, 0, 0)),
                  pl.BlockSpec((None, H, hd), lambda b: (b, 0, 0))],
        out_specs=pl.BlockSpec((None, MOBA_TOPK, H, hd), lambda b: (b, 0, 0, 0)),
        out_shape=jax.ShapeDtypeStruct((B, MOBA_TOPK, H, hd), jnp.int32),
        compiler_params=_params(("parallel",), [((n_blocks, H, hd), F32), ((H, hd), F32), ((MOBA_TOPK, H, hd), F32)]),
        name="sample_select",
    )(ksum, q)
    return sel[:, :, :, 0]


HEAD_GROUP = 8


def _sample_attn_kernel(pages_ref, q_ref, kn_ref, vn_ref, *refs, scale, n_sel_pages):
    kp_refs, vp_refs = refs[:n_sel_pages], refs[n_sel_pages:2 * n_sel_pages]
    o_ref, res_ref = refs[2 * n_sel_pages:]
    q = q_ref[...] * scale
    s_new = jnp.sum(q * kn_ref[...], axis=-1, keepdims=True)
    s_pages = [jnp.sum(r[...] * q[None, :, :], axis=-1, keepdims=True) for r in kp_refs]
    m = s_new
    for s in s_pages:
        m = jnp.maximum(m, jnp.max(s, axis=0))
    l = jnp.exp(s_new - m)
    acc = l * vn_ref[...]
    for s, r in zip(s_pages, vp_refs):
        p = jnp.exp(s - m[None, :, :])
        l = l + jnp.sum(p, axis=0)
        acc = acc + jnp.sum(p * r[...], axis=0)
    res_ref[...] = acc / l
    row = pl.ds(pl.program_id(1) % HEAD_GROUP, 1)
    o_ref[row, :] = res_ref[row, :]


def _sample_attn(pages, q, k_new, v_new, cache_k, cache_v, layer, *, n_sel_pages):
    B, H, hd = q.shape
    page = cache_k.shape[2]
    assert H % HEAD_GROUP == 0
    vec = pl.BlockSpec((None, HEAD_GROUP, hd), lambda b, h, pg: (b, h // HEAD_GROUP, 0))

    def tile(t):
        return pl.BlockSpec((None, None, page, HEAD_GROUP, hd),
                            lambda b, h, pg: (layer, pg[(b * H + h) * n_sel_pages + t], 0, h // HEAD_GROUP, 0))

    tiles = [tile(t) for t in range(n_sel_pages)]
    return pl.pallas_call(
        functools.partial(_sample_attn_kernel, scale=hd ** -0.5, n_sel_pages=n_sel_pages),
        grid_spec=pltpu.PrefetchScalarGridSpec(
            num_scalar_prefetch=1,
            grid=(B, H),
            in_specs=[vec, vec, vec] + tiles + tiles,
            out_specs=vec,
            scratch_shapes=[pltpu.VMEM((HEAD_GROUP, hd), F32)],
        ),
        out_shape=jax.ShapeDtypeStruct((B, H, hd), F32),
        compiler_params=_params(("parallel", "arbitrary"),
                                [((HEAD_GROUP, hd), F32)] * 4 + [((page, HEAD_GROUP, hd), F32)] * (2 * n_sel_pages)),
        name="sample_attn",
    )(pages, q, k_new, v_new, *([cache_k] * n_sel_pages), *([cache_v] * n_sel_pages))


def _rope_tables(pos, head_dim):
    half = head_dim // 2
    inv = ROPE_THETA ** (-jnp.arange(half, dtype=F32) / half)
    ang = pos.astype(F32)[:, None] * inv[None, :]
    cos, sin = jnp.cos(ang), jnp.sin(ang)
    return jnp.concatenate([cos, cos], axis=1), jnp.concatenate([-sin, sin], axis=1)


def kernel(x_prompt, x_sample, cache_k, cache_v, state_pool, page_table, norm_mix, norm_ffn, w_qkv, w_o, q_norm, k_norm, w_pool, pool_scale, w_gate_up, w_down):
    Bp, S, D = x_prompt.shape
    Bd, L, _ = x_sample.shape
    depth = norm_mix.shape[0]
    n_layers_attn, n_phys, page_size, n_heads, hd = cache_k.shape
    n_pages = page_table.shape[1]
    past = n_pages * page_size
    assert n_heads == N_HEADS and L == 1 and MOBA_BLOCK % page_size == 0 and past % MOBA_BLOCK == 0
    pages_per_block = MOBA_BLOCK // page_size
    n_sel_pages = MOBA_TOPK * pages_per_block

    w_qkv_b, w_o_b = w_qkv.astype(BF16), w_o.astype(BF16)
    w_gu_b, w_dn_b, w_pool_b = w_gate_up.astype(BF16), w_down.astype(BF16), w_pool.astype(BF16)
    cos_p, sin_p = _rope_tables(jnp.tile(jnp.arange(S), Bp), hd)
    cos_s, sin_s = _rope_tables(jnp.tile(past + jnp.arange(L), Bd), hd)

    Mp, Ms = Bp * S, Bd * L
    xp = x_prompt.reshape(Mp, D)
    xs = x_sample.reshape(Ms, D)
    tm_p, tm_s = 512, Ms
    kp_list, vp_list, ks_list, vs_list, sp_list, ss_list = [], [], [], [], [], []
    for i in range(depth):
        g_mix = norm_mix[i][None, :]
        g_ffn = norm_ffn[i][None, :]
        if i % 2 == 0:
            a = i // 2
            qg, kg = q_norm[a][None, :], k_norm[a][None, :]
            q, k, v = _qkv(xp, g_mix, w_qkv_b[a], qg, kg, cos_p, sin_p, tm=tm_p, tn=512)
            o = _moba_prompt(q.reshape(Bp, S, D), k.reshape(Bp, S, D), v.reshape(Bp, S, D))
            xp = _oproj(o.reshape(Mp, D), w_o_b[a], xp, tm=tm_p)
            kp_list.append(k.reshape(Bp, S, N_HEADS, hd))
            vp_list.append(v.reshape(Bp, S, N_HEADS, hd))
            qs, kn, vn = _qkv(xs, g_mix, w_qkv_b[a], qg, kg, cos_s, sin_s, tm=tm_s, tn=512)
            ksum = _block_ksum(cache_k, page_table, a, pages_per_block=pages_per_block)
            sel = _sample_select(ksum, qs.reshape(Bd, N_HEADS, hd))[:, :MOBA_TOPK, :]
            logical = sel[:, :, :, None] * pages_per_block + jnp.arange(pages_per_block)
            logical = logical.transpose(0, 2, 1, 3).reshape(Bd, N_HEADS * n_sel_pages)
            pages = jnp.take_along_axis(page_table, logical, axis=1).reshape(-1)
            os_ = _sample_attn(pages, qs.reshape(Bd, N_HEADS, hd), kn.reshape(Bd, N_HEADS, hd),
                               vn.reshape(Bd, N_HEADS, hd), cache_k, cache_v, a, n_sel_pages=n_sel_pages)
            xs = _oproj(os_.reshape(Ms, D).astype(BF16), w_o_b[a], xs, tm=tm_s)
            ks_list.append(kn.reshape(Bd, L, N_HEADS, hd))
            vs_list.append(vn.reshape(Bd, L, N_HEADS, hd))
        else:
            p = i // 2
            sc = pool_scale[p][None, :]
            y, tail = _pool_prompt(xp.reshape(Bp, S, D), g_mix, w_pool_b[p], sc, tm=tm_p)
            xp = y.reshape(Mp, D)
            sp_list.append(tail[:, POOL_CARRY_ROWS - POOL_STATE_LEN:, :])
            xs, hs = _pool_sample(xs, state_pool[p], g_mix, w_pool_b[p], sc, start_pos=past)
            ss_list.append(jnp.concatenate([state_pool[p][:, 1:, :], hs[:, None, :]], axis=1))
        xp = _ffn(xp, g_ffn, w_gu_b[i], w_dn_b[i], tm=tm_p, tf=512)
        xs = _ffn(xs, g_ffn, w_gu_b[i], w_dn_b[i], tm=tm_s, tf=512)
    return (xp.reshape(Bp, S, D), xs.reshape(Bd, L, D), jnp.stack(kp_list), jnp.stack(vp_list),
            jnp.stack(ks_list), jnp.stack(vs_list), jnp.stack(sp_list), jnp.stack(ss_list))
```

```python
import functools

import jax
import jax.numpy as jnp
import numpy as np
from jax import lax
from jax.experimental import pallas as pl
from jax.experimental.pallas import tpu as pltpu

F32 = jnp.float32
BF16 = jnp.bfloat16

N_HEADS = 16
MOBA_BLOCK = 256
MOBA_TOPK = 3
ROPE_THETA = 10000.0
POOL_WINDOWS = (2, 4, 8, 16)
POOL_STATE_LEN = max(POOL_WINDOWS) - 1
POOL_CARRY_ROWS = 16
EPS = 1e-6
NEG_INF = float("-inf")

V7X_VMEM_BYTES = 64 * 1024 * 1024
V7X_LANES = 128
TEMPORARIES_VMEM_BYTES = 12 * 1024 * 1024


def _nbytes(shape, dtype):
    return int(np.prod(shape)) * jnp.dtype(dtype).itemsize


def _vmem_limit(pipelined, scratch=()):
    total = sum(2 * _nbytes(s, d) for s, d in pipelined) + sum(_nbytes(s, d) for s, d in scratch)
    return min(total + TEMPORARIES_VMEM_BYTES, V7X_VMEM_BYTES - 4 * 1024 * 1024)


def _params(semantics, pipelined, scratch=()):
    return pltpu.CompilerParams(dimension_semantics=semantics, vmem_limit_bytes=_vmem_limit(pipelined, scratch))


def _rms(x, g):
    return x * lax.rsqrt(jnp.mean(x * x, axis=-1, keepdims=True) + EPS) * g


def _dot(a, b):
    return jnp.dot(a, b, preferred_element_type=F32)


def _qkv_kernel(x_ref, g_ref, wq_ref, wk_ref, wv_ref, qg_ref, kg_ref, cos_ref, sin_ref,
                q_ref, k_ref, v_ref, xn_ref, *, head_dim):
    @pl.when(pl.program_id(1) == 0)
    def _():
        xn_ref[...] = _rms(x_ref[...], g_ref[...]).astype(BF16)

    xn = xn_ref[...]
    cos = cos_ref[...]
    sin = sin_ref[...]
    heads_per_tile = q_ref.shape[1] // head_dim

    def norm_rope(y, gain, out_ref):
        for h in range(heads_per_tile):
            yh = y[:, h * head_dim:(h + 1) * head_dim]
            yh = _rms(yh, gain)
            rot = pltpu.roll(yh, head_dim // 2, axis=1)
            out_ref[:, h * head_dim:(h + 1) * head_dim] = yh * cos + rot * sin

    norm_rope(_dot(xn, wq_ref[...]), qg_ref[...], q_ref)
    norm_rope(_dot(xn, wk_ref[...]), kg_ref[...], k_ref)
    v_ref[...] = _dot(xn, wv_ref[...])


def _qkv(x, g, w_qkv, layer, qg, kg, cos, sin, *, tm, tn):
    M, D = x.shape
    hd = D // N_HEADS
    nn = D // tn
    row = lambda m, n: (m, 0)
    col = lambda off: (lambda m, n: (layer, 0, n + off * nn))
    out = pl.BlockSpec((tm, tn), lambda m, n: (m, n))
    blocks = [((tm, D), F32), ((D, tn), BF16), ((D, tn), BF16), ((D, tn), BF16),
              ((tm, hd), F32), ((tm, hd), F32), ((tm, tn), F32), ((tm, tn), F32), ((tm, tn), F32)]
    return pl.pallas_call(
        functools.partial(_qkv_kernel, head_dim=hd),
        grid=(M // tm, nn),
        in_specs=[pl.BlockSpec((tm, D), row), pl.BlockSpec((1, D), lambda m, n: (0, 0)),
                  pl.BlockSpec((None, D, tn), col(0)), pl.BlockSpec((None, D, tn), col(1)),
                  pl.BlockSpec((None, D, tn), col(2)),
                  pl.BlockSpec((1, hd), lambda m, n: (0, 0)), pl.BlockSpec((1, hd), lambda m, n: (0, 0)),
                  pl.BlockSpec((tm, hd), row), pl.BlockSpec((tm, hd), row)],
        out_specs=[out, out, out],
        out_shape=[jax.ShapeDtypeStruct((M, D), F32)] * 3,
        scratch_shapes=[pltpu.VMEM((tm, D), BF16)],
        compiler_params=_params(("parallel", "arbitrary"), blocks, [((tm, D), BF16)]),
        name="qkv_proj",
    )(x, g, w_qkv, w_qkv, w_qkv, qg, kg, cos, sin)


def _topk_rank(g, blk_id):
    rank = jnp.zeros(g.shape, jnp.int32)
    for m in range(g.shape[0]):
        gm = g[m:m + 1]
        beats = (gm > g) | ((gm == g) & (m < blk_id))
        rank = rank + beats.astype(jnp.int32)
    return rank


Q_GROUP = 4


def _moba_prompt_kernel(q_ref, k_ref, v_ref, o_ref, kb_ref, vt_ref, km_ref, bias_ref, s_ref, acc_ref, *, scale):
    blk = MOBA_BLOCK
    seq, hd = k_ref.shape
    nb = seq // blk
    tq = q_ref.shape[0]
    g = pl.program_id(2)
    first = g * Q_GROUP

    @pl.when(g == 0)
    def _():
        kb_ref[...] = k_ref[...].astype(BF16)
        for n in range(nb):
            km_ref[n:n + 1, :] = jnp.mean(k_ref[n * blk:(n + 1) * blk, :], axis=0, keepdims=True)
            vt_ref[n] = v_ref[n * blk:(n + 1) * blk, :].T.astype(BF16)

    q = q_ref[...]
    gate = lax.dot_general(km_ref[...], q, (((1,), (1,)), ((), ())),
                           precision=lax.Precision.HIGHEST, preferred_element_type=F32)
    blk_id = lax.broadcasted_iota(jnp.int32, gate.shape, 0)
    own = first + lax.broadcasted_iota(jnp.int32, gate.shape, 1) // blk
    past = blk_id < own
    rank = _topk_rank(jnp.where(past, gate, NEG_INF), blk_id)
    bias_ref[...] = jnp.where((rank < MOBA_TOPK) & past, 0.0, NEG_INF)

    qt = (q * scale).T.astype(BF16)

    def scores(n):
        return _dot(kb_ref[pl.ds(pl.multiple_of(n * blk, blk), blk), :], qt)

    def sublane_groups(x):
        return x.reshape(blk // 8, 8, tq)

    def past_scores(c, mx):
        for u in range(Q_GROUP):
            n = c * Q_GROUP + u
            s = scores(n) + bias_ref[pl.ds(n, 1), :]
            s_ref[n] = s
            mx = jnp.maximum(mx, jnp.max(sublane_groups(s), axis=0))
        return mx

    mx = lax.fori_loop(0, g, past_scores, jnp.full((8, tq), NEG_INF, F32))

    kpos = lax.broadcasted_iota(jnp.int32, (blk, tq), 0)
    qcol = lax.broadcasted_iota(jnp.int32, (blk, tq), 1)
    for u in range(Q_GROUP):
        n = first + u
        s = scores(n)
        in_own = (qcol >= u * blk) & (qcol < (u + 1) * blk)
        s = jnp.where(in_own, jnp.where(kpos <= qcol - u * blk, s, NEG_INF), s + bias_ref[pl.ds(n, 1), :])
        s_ref[n] = s
        mx = jnp.maximum(mx, jnp.max(sublane_groups(s), axis=0))
    m = jnp.max(mx, axis=0, keepdims=True)

    def weigh(n, l8):
        p = jnp.exp(s_ref[n] - m)
        acc_ref[...] += _dot(vt_ref[n], p.astype(BF16))
        return l8 + jnp.sum(sublane_groups(p), axis=0)

    def past_weigh(c, l8):
        for u in range(Q_GROUP):
            l8 = weigh(c * Q_GROUP + u, l8)
        return l8

    acc_ref[...] = jnp.zeros(acc_ref.shape, F32)
    l8 = lax.fori_loop(0, g, past_weigh, jnp.zeros((8, tq), F32))
    for u in range(Q_GROUP):
        l8 = weigh(first + u, l8)
    l = jnp.sum(l8, axis=0, keepdims=True)
    o_ref[...] = (acc_ref[...] / l).T.astype(o_ref.dtype)


def _moba_prompt(q, k, v):
    B, S, D = q.shape
    hd = D // N_HEADS
    tq = Q_GROUP * MOBA_BLOCK
    assert hd == V7X_LANES and S % tq == 0
    nb = S // MOBA_BLOCK
    tile = pl.BlockSpec((None, tq, hd), lambda b, h, g: (b, g, h))
    full = pl.BlockSpec((None, S, hd), lambda b, h, g: (b, 0, h))
    scratch = [((S, hd), BF16), ((nb, hd, MOBA_BLOCK), BF16), ((nb, hd), F32), ((nb, tq), F32),
               ((nb, MOBA_BLOCK, tq), F32), ((hd, tq), F32)]
    return pl.pallas_call(
        functools.partial(_moba_prompt_kernel, scale=hd ** -0.5),
        grid=(B, N_HEADS, S // tq),
        in_specs=[tile, full, full],
        out_specs=tile,
        out_shape=jax.ShapeDtypeStruct((B, S, D), BF16),
        scratch_shapes=[pltpu.VMEM(s, d) for s, d in scratch],
        compiler_params=_params(("parallel", "parallel", "arbitrary"),
                                [((tq, hd), F32), ((S, hd), F32), ((S, hd), F32), ((tq, hd), BF16)], scratch),
        name="moba_prompt",
    )(q, k, v)


def _oproj_kernel(o_ref, w_ref, x_ref, y_ref):
    y_ref[...] = x_ref[...] + _dot(o_ref[...], w_ref[...])


def _oproj(o, w, layer, x, *, tm):
    M, D = x.shape
    row = pl.BlockSpec((tm, D), lambda m: (m, 0))
    return pl.pallas_call(
        _oproj_kernel,
        grid=(M // tm,),
        in_specs=[row, pl.BlockSpec((None, D, D), lambda m: (layer, 0, 0)), row],
        out_specs=row,
        out_shape=jax.ShapeDtypeStruct((M, D), F32),
        compiler_params=_params(("parallel",), [((tm, D), BF16), ((D, D), BF16), ((tm, D), F32), ((tm, D), F32)]),
        name="out_proj",
    )(o, w, x)


def _ffn_kernel(x_ref, g_ref, wg_ref, wu_ref, wd_ref, y_ref, xn_ref):
    @pl.when(pl.program_id(1) == 0)
    def _():
        x = x_ref[...]
        xn_ref[...] = _rms(x, g_ref[...]).astype(BF16)
        y_ref[...] = x

    xn = xn_ref[...]
    gate = _dot(xn, wg_ref[...])
    up = _dot(xn, wu_ref[...])
    h = (gate / (1.0 + jnp.exp(-gate)) * up).astype(BF16)
    y_ref[...] += _dot(h, wd_ref[...])


def _ffn(x, g, w_gu, w_dn, layer, *, tm, tf):
    M, D = x.shape
    d_ff = w_dn.shape[1]
    nf = d_ff // tf
    assert nf * tf == d_ff
    row = pl.BlockSpec((tm, D), lambda m, f: (m, 0))
    blocks = [((tm, D), F32), ((D, tf), BF16), ((D, tf), BF16), ((tf, D), BF16), ((tm, D), F32)]
    return pl.pallas_call(
        _ffn_kernel,
        grid=(M // tm, nf),
        in_specs=[row, pl.BlockSpec((1, D), lambda m, f: (0, 0)),
                  pl.BlockSpec((None, D, tf), lambda m, f: (layer, 0, f)),
                  pl.BlockSpec((None, D, tf), lambda m, f: (layer, 0, f + nf)),
                  pl.BlockSpec((None, tf, D), lambda m, f: (layer, f, 0))],
        out_specs=row,
        out_shape=jax.ShapeDtypeStruct((M, D), F32),
        scratch_shapes=[pltpu.VMEM((tm, D), BF16)],
        compiler_params=_params(("parallel", "arbitrary"), blocks, [((tm, D), BF16)]),
        name="swiglu_ffn",
    )(x, g, w_gu, w_gu, w_dn)


def _pool_prompt_kernel(x_ref, g_ref, w_ref, sc_ref, y_ref, tail_ref, ext_ref):
    tm, D = x_ref.shape
    carry = POOL_CARRY_ROWS
    gd = D // len(POOL_WINDOWS)
    si = pl.program_id(1)

    @pl.when(si == 0)
    def _():
        ext_ref[0:carry, :] = jnp.zeros((carry, D), F32)

    x = x_ref[...]
    h = _rms(x, g_ref[...])
    ext_ref[carry:carry + tm, :] = h
    pos = si * tm + lax.broadcasted_iota(jnp.int32, (tm, 1), 0)
    for gi, w in enumerate(POOL_WINDOWS):
        lo, hi = gi * gd, (gi + 1) * gd
        hg = h[:, lo:hi]
        win = hg
        for back in range(1, w):
            win = win + ext_ref[carry - back:carry - back + tm, lo:hi]
        cnt = jnp.minimum(pos + 1, w).astype(F32)
        d = (win / cnt - hg).astype(BF16)
        y_ref[:, lo:hi] = x[:, lo:hi] + _dot(d, w_ref[gi]) * sc_ref[:, lo:hi]
    last = h[tm - carry:tm, :]
    ext_ref[0:carry, :] = last
    tail_ref[...] = last


def _pool_prompt(x, g, w_pool, layer, sc, *, tm):
    B, S, D = x.shape
    gd = D // len(POOL_WINDOWS)
    assert S % tm == 0 and tm >= POOL_CARRY_ROWS
    row = pl.BlockSpec((None, tm, D), lambda b, s: (b, s, 0))
    vec = pl.BlockSpec((1, D), lambda b, s: (0, 0))
    return pl.pallas_call(
        _pool_prompt_kernel,
        grid=(B, S // tm),
        in_specs=[row, vec, pl.BlockSpec((None, len(POOL_WINDOWS), gd, gd), lambda b, s: (layer, 0, 0, 0)), vec],
        out_specs=[row, pl.BlockSpec((None, POOL_CARRY_ROWS, D), lambda b, s: (b, 0, 0))],
        out_shape=[jax.ShapeDtypeStruct((B, S, D), F32), jax.ShapeDtypeStruct((B, POOL_CARRY_ROWS, D), F32)],
        scratch_shapes=[pltpu.VMEM((POOL_CARRY_ROWS + tm, D), F32)],
        compiler_params=_params(("parallel", "arbitrary"),
                                [((tm, D), F32), ((len(POOL_WINDOWS), gd, gd), BF16), ((tm, D), F32)],
                                [((POOL_CARRY_ROWS + tm, D), F32)]),
        name="pool_prompt",
    )(x, g, w_pool, sc)


def _pool_sample_kernel(x_ref, pre_ref, g_ref, w_ref, sc_ref, y_ref, h_ref, *, counts):
    D = x_ref.shape[1]
    gd = D // len(POOL_WINDOWS)
    x = x_ref[...]
    h = _rms(x, g_ref[...])
    h_ref[...] = h
    n_pre = pre_ref.shape[1]
    for gi, w in enumerate(POOL_WINDOWS):
        lo, hi = gi * gd, (gi + 1) * gd
        hg = h[:, lo:hi]
        win = hg
        for back in range(1, w):
            win = win + pre_ref[:, n_pre - back, lo:hi]
        d = (win / counts[gi] - hg).astype(BF16)
        y_ref[:, lo:hi] = x[:, lo:hi] + _dot(d, w_ref[gi]) * sc_ref[:, lo:hi]


def _pool_sample(x, prefix, g, w_pool, sc, *, start_pos):
    B, D = x.shape
    counts = tuple(float(min(start_pos + 1, w)) for w in POOL_WINDOWS)
    return pl.pallas_call(
        functools.partial(_pool_sample_kernel, counts=counts),
        out_shape=[jax.ShapeDtypeStruct((B, D), F32), jax.ShapeDtypeStruct((B, D), F32)],
        name="pool_sample",
    )(x, prefix, g, w_pool, sc)


def _block_ksum_kernel(pt_ref, *refs, pages_per_block):
    page_refs, o_ref = refs[:-1], refs[-1]
    for i in range(o_ref.shape[0]):
        total = jnp.sum(page_refs[i * pages_per_block][...], axis=0)
        for r in page_refs[i * pages_per_block + 1:(i + 1) * pages_per_block]:
            total = total + jnp.sum(r[...], axis=0)
        o_ref[i] = total


KSUM_BLOCKS_PER_STEP = 2


def _block_ksum(cache_k, page_table, layer, *, pages_per_block):
    _, _, page, H, hd = cache_k.shape
    B, n_pages = page_table.shape
    n_blocks = n_pages // pages_per_block
    pages_per_step = KSUM_BLOCKS_PER_STEP * pages_per_block
    assert n_blocks % KSUM_BLOCKS_PER_STEP == 0

    def page_spec(i):
        return pl.BlockSpec((None, None, page, H, hd),
                            lambda b, n, pt: (layer, pt[b, n * pages_per_step + i], 0, 0, 0))

    return pl.pallas_call(
        functools.partial(_block_ksum_kernel, pages_per_block=pages_per_block),
        grid_spec=pltpu.PrefetchScalarGridSpec(
            num_scalar_prefetch=1,
            grid=(B, n_blocks // KSUM_BLOCKS_PER_STEP),
            in_specs=[page_spec(i) for i in range(pages_per_step)],
            out_specs=pl.BlockSpec((None, KSUM_BLOCKS_PER_STEP, H, hd), lambda b, n, pt: (b, n, 0, 0)),
        ),
        out_shape=jax.ShapeDtypeStruct((B, n_blocks, H, hd), F32),
        compiler_params=_params(("parallel", "parallel"),
                                [((page, H, hd), F32)] * pages_per_step + [((KSUM_BLOCKS_PER_STEP, H, hd), F32)]),
        name="block_ksum",
    )(page_table, *([cache_k] * pages_per_step))


def _sample_select_kernel(ks_ref, q_ref, sel_ref):
    k_mean = ks_ref[...] * (1.0 / MOBA_BLOCK)
    gate = jnp.sum(k_mean * q_ref[...][None, :, :], axis=-1, keepdims=True)
    blk_id = lax.broadcasted_iota(jnp.int32, gate.shape, 0)
    rank = _topk_rank(gate, blk_id)
    for r in range(MOBA_TOPK):
        picked = jnp.sum(jnp.where(rank == r, blk_id, 0), axis=0)
        sel_ref[r] = jnp.broadcast_to(picked, sel_ref.shape[1:])


def _sample_select(ksum, q):
    B, n_blocks, H, hd = ksum.shape
    assert n_blocks >= MOBA_TOPK
    sel = pl.pallas_call(
        _sample_select_kernel,
        grid=(B,),
        in_specs=[pl.BlockSpec((None, n_blocks, H, hd), lambda b: (b, 0, 0, 0)),
                  pl.BlockSpec((None, H, hd), lambda b: (b, 0, 0))],
        out_specs=pl.BlockSpec((None, MOBA_TOPK, H, hd), lambda b: (b, 0, 0, 0)),
        out_shape=jax.ShapeDtypeStruct((B, MOBA_TOPK, H, hd), jnp.int32),
        compiler_params=_params(("parallel",), [((n_blocks, H, hd), F32), ((H, hd), F32), ((MOBA_TOPK, H, hd), F32)]),
        name="sample_select",
    )(ksum, q)
    return sel[:, :, :, 0]


def _sample_attn_kernel(pages_ref, q_ref, kn_ref, vn_ref, ck_hbm, cv_hbm, o_ref, kbuf, vbuf, ksem, vsem,
                        *, scale, layer, n_sel_pages):
    b = pl.program_id(0)
    n_heads = q_ref.shape[0]
    page_size = kbuf.shape[1] // n_sel_pages

    def copies(h, t):
        page = pages_ref[(b * n_heads + h) * n_sel_pages + t]
        rows = pl.ds(t * page_size, page_size)
        slot = h * n_sel_pages + t
        return (pltpu.make_async_copy(ck_hbm.at[layer, page, :, h, :], kbuf.at[h, rows], ksem.at[slot]),
                pltpu.make_async_copy(cv_hbm.at[layer, page, :, h, :], vbuf.at[h, rows], vsem.at[slot]))

    def start_head(h, carry):
        for t in range(n_sel_pages):
            for c in copies(h, t):
                c.start()
        return carry

    lax.fori_loop(0, n_heads, start_head, 0)

    def head(h, carry):
        for t in range(n_sel_pages):
            for c in copies(h, t):
                c.wait()
        row = pl.ds(h, 1)
        q = q_ref[row, :] * scale
        s_new = jnp.sum(q * kn_ref[row, :], axis=-1, keepdims=True)
        s = jnp.sum(kbuf[h] * q, axis=-1, keepdims=True)
        m = jnp.maximum(s_new, jnp.max(s, axis=0, keepdims=True))
        p_new = jnp.exp(s_new - m)
        p = jnp.exp(s - m)
        l = p_new + jnp.sum(p, axis=0, keepdims=True)
        acc = p_new * vn_ref[row, :] + jnp.sum(p * vbuf[h], axis=0, keepdims=True)
        o_ref[row, :] = acc / l
        return carry

    lax.fori_loop(0, n_heads, head, 0)


def _sample_attn(pages, q, k_new, v_new, cache_k, cache_v, layer, *, n_sel_pages):
    B, H, hd = q.shape
    page = cache_k.shape[2]
    vec = pl.BlockSpec((None, H, hd), lambda b, pg: (b, 0, 0))
    hbm = pl.BlockSpec(memory_space=pl.ANY)
    bufs = [((H, n_sel_pages * page, hd), F32)] * 2
    return pl.pallas_call(
        functools.partial(_sample_attn_kernel, scale=hd ** -0.5, layer=layer, n_sel_pages=n_sel_pages),
        grid_spec=pltpu.PrefetchScalarGridSpec(
            num_scalar_prefetch=1,
            grid=(B,),
            in_specs=[vec, vec, vec, hbm, hbm],
            out_specs=vec,
            scratch_shapes=[pltpu.VMEM(s_, d_) for s_, d_ in bufs] + [pltpu.SemaphoreType.DMA((H * n_sel_pages,))] * 2,
        ),
        out_shape=jax.ShapeDtypeStruct((B, H, hd), F32),
        compiler_params=_params(("arbitrary",), [((H, hd), F32)] * 4, bufs),
        name="sample_attn",
    )(pages, q, k_new, v_new, cache_k, cache_v)


def _rope_tables(pos, head_dim):
    half = head_dim // 2
    inv = ROPE_THETA ** (-jnp.arange(half, dtype=F32) / half)
    ang = pos.astype(F32)[:, None] * inv[None, :]
    cos, sin = jnp.cos(ang), jnp.sin(ang)
    return jnp.concatenate([cos, cos], axis=1), jnp.concatenate([-sin, sin], axis=1)


def kernel(x_prompt, x_sample, cache_k, cache_v, state_pool, page_table, norm_mix, norm_ffn, w_qkv, w_o, q_norm, k_norm, w_pool, pool_scale, w_gate_up, w_down):
    Bp, S, D = x_prompt.shape
    Bd, L, _ = x_sample.shape
    depth = norm_mix.shape[0]
    n_layers_attn, n_phys, page_size, n_heads, hd = cache_k.shape
    n_pages = page_table.shape[1]
    past = n_pages * page_size
    assert n_heads == N_HEADS and L == 1 and MOBA_BLOCK % page_size == 0 and past % MOBA_BLOCK == 0
    pages_per_block = MOBA_BLOCK // page_size
    n_sel_pages = MOBA_TOPK * pages_per_block

    w_qkv_b, w_o_b = w_qkv.astype(BF16), w_o.astype(BF16)
    w_gu_b, w_dn_b, w_pool_b = w_gate_up.astype(BF16), w_down.astype(BF16), w_pool.astype(BF16)
    cos_p, sin_p = _rope_tables(jnp.tile(jnp.arange(S), Bp), hd)
    cos_s, sin_s = _rope_tables(jnp.tile(past + jnp.arange(L), Bd), hd)

    Mp, Ms = Bp * S, Bd * L
    xp = x_prompt.reshape(Mp, D)
    xs = x_sample.reshape(Ms, D)
    tm_p, tm_s = 512, Ms
    kp_list, vp_list, ks_list, vs_list, sp_list, ss_list = [], [], [], [], [], []
    ksums = [_block_ksum(cache_k, page_table, a, pages_per_block=pages_per_block) for a in range(n_layers_attn)]
    for i in range(depth):
        g_mix = norm_mix[i][None, :]
        g_ffn = norm_ffn[i][None, :]
        if i % 2 == 0:
            a = i // 2
            qg, kg = q_norm[a][None, :], k_norm[a][None, :]
            q, k, v = _qkv(xp, g_mix, w_qkv_b, a, qg, kg, cos_p, sin_p, tm=tm_p, tn=512)
            o = _moba_prompt(q.reshape(Bp, S, D), k.reshape(Bp, S, D), v.reshape(Bp, S, D))
            xp = _oproj(o.reshape(Mp, D), w_o_b, a, xp, tm=tm_p)
            kp_list.append(k.reshape(Bp, S, N_HEADS, hd))
            vp_list.append(v.reshape(Bp, S, N_HEADS, hd))
            qs, kn, vn = _qkv(xs, g_mix, w_qkv_b, a, qg, kg, cos_s, sin_s, tm=tm_s, tn=512)
            sel = _sample_select(ksums[a], qs.reshape(Bd, N_HEADS, hd))
            logical = sel[:, :, :, None] * pages_per_block + jnp.arange(pages_per_block)
            logical = logical.transpose(0, 2, 1, 3).reshape(Bd, N_HEADS * n_sel_pages)
            pages = jnp.take_along_axis(page_table, logical, axis=1).reshape(-1)
            os_ = _sample_attn(pages, qs.reshape(Bd, N_HEADS, hd), kn.reshape(Bd, N_HEADS, hd),
                               vn.reshape(Bd, N_HEADS, hd), cache_k, cache_v, a, n_sel_pages=n_sel_pages)
            xs = _oproj(os_.reshape(Ms, D).astype(BF16), w_o_b, a, xs, tm=tm_s)
            ks_list.append(kn.reshape(Bd, L, N_HEADS, hd))
            vs_list.append(vn.reshape(Bd, L, N_HEADS, hd))
        else:
            p = i // 2
            sc = pool_scale[p][None, :]
            y, tail = _pool_prompt(xp.reshape(Bp, S, D), g_mix, w_pool_b, p, sc, tm=tm_p)
            xp = y.reshape(Mp, D)
            sp_list.append(tail[:, POOL_CARRY_ROWS - POOL_STATE_LEN:, :])
            xs, hs = _pool_sample(xs, state_pool[p], g_mix, w_pool_b[p], sc, start_pos=past)
            ss_list.append(jnp.concatenate([state_pool[p][:, 1:, :], hs[:, None, :]], axis=1))
        xp = _ffn(xp, g_ffn, w_gu_b, w_dn_b, i, tm=tm_p, tf=512)
        xs = _ffn(xs, g_ffn, w_gu_b, w_dn_b, i, tm=tm_s, tf=512)
    return (xp.reshape(Bp, S, D), xs.reshape(Bd, L, D), jnp.stack(kp_list), jnp.stack(vp_list),
            jnp.stack(ks_list), jnp.stack(vs_list), jnp.stack(sp_list), jnp.stack(ss_list))
```

```python
import functools

import jax
import jax.numpy as jnp
import numpy as np
from jax import lax
from jax.experimental import pallas as pl
from jax.experimental.pallas import tpu as pltpu

F32 = jnp.float32
BF16 = jnp.bfloat16

N_HEADS = 16
MOBA_BLOCK = 256
MOBA_TOPK = 3
ROPE_THETA = 10000.0
POOL_WINDOWS = (2, 4, 8, 16)
POOL_STATE_LEN = max(POOL_WINDOWS) - 1
POOL_CARRY_ROWS = 16
EPS = 1e-6
NEG_INF = float("-inf")

V7X_VMEM_BYTES = 64 * 1024 * 1024
V7X_LANES = 128
TEMPORARIES_VMEM_BYTES = 12 * 1024 * 1024


def _nbytes(shape, dtype):
    return int(np.prod(shape)) * jnp.dtype(dtype).itemsize


def _vmem_limit(pipelined, scratch=()):
    total = sum(2 * _nbytes(s, d) for s, d in pipelined) + sum(_nbytes(s, d) for s, d in scratch)
    return min(total + TEMPORARIES_VMEM_BYTES, V7X_VMEM_BYTES - 4 * 1024 * 1024)


def _params(semantics, pipelined, scratch=()):
    return pltpu.CompilerParams(dimension_semantics=semantics, vmem_limit_bytes=_vmem_limit(pipelined, scratch))


def _rms(x, g):
    return x * lax.rsqrt(jnp.mean(x * x, axis=-1, keepdims=True) + EPS) * g


def _dot(a, b):
    return jnp.dot(a, b, preferred_element_type=F32)


def _qkv_kernel(x_ref, g_ref, wq_ref, wk_ref, wv_ref, qg_ref, kg_ref, cos_ref, sin_ref,
                q_ref, k_ref, v_ref, xn_ref, *, head_dim):
    @pl.when(pl.program_id(1) == 0)
    def _():
        xn_ref[...] = _rms(x_ref[...], g_ref[...]).astype(BF16)

    xn = xn_ref[...]
    cos = cos_ref[...]
    sin = sin_ref[...]
    heads_per_tile = q_ref.shape[1] // head_dim

    def norm_rope(y, gain, out_ref):
        for h in range(heads_per_tile):
            yh = y[:, h * head_dim:(h + 1) * head_dim]
            yh = _rms(yh, gain)
            rot = pltpu.roll(yh, head_dim // 2, axis=1)
            out_ref[:, h * head_dim:(h + 1) * head_dim] = yh * cos + rot * sin

    norm_rope(_dot(xn, wq_ref[...]), qg_ref[...], q_ref)
    norm_rope(_dot(xn, wk_ref[...]), kg_ref[...], k_ref)
    v_ref[...] = _dot(xn, wv_ref[...])


def _qkv(x, g, w_qkv, layer, qg, kg, cos, sin, *, tm, tn):
    M, D = x.shape
    hd = D // N_HEADS
    nn = D // tn
    row = lambda m, n: (m, 0)
    col = lambda off: (lambda m, n: (layer, 0, n + off * nn))
    out = pl.BlockSpec((tm, tn), lambda m, n: (m, n))
    blocks = [((tm, D), F32), ((D, tn), BF16), ((D, tn), BF16), ((D, tn), BF16),
              ((tm, hd), F32), ((tm, hd), F32), ((tm, tn), F32), ((tm, tn), F32), ((tm, tn), F32)]
    return pl.pallas_call(
        functools.partial(_qkv_kernel, head_dim=hd),
        grid=(M // tm, nn),
        in_specs=[pl.BlockSpec((tm, D), row), pl.BlockSpec((1, D), lambda m, n: (0, 0)),
                  pl.BlockSpec((None, D, tn), col(0)), pl.BlockSpec((None, D, tn), col(1)),
                  pl.BlockSpec((None, D, tn), col(2)),
                  pl.BlockSpec((1, hd), lambda m, n: (0, 0)), pl.BlockSpec((1, hd), lambda m, n: (0, 0)),
                  pl.BlockSpec((tm, hd), row), pl.BlockSpec((tm, hd), row)],
        out_specs=[out, out, out],
        out_shape=[jax.ShapeDtypeStruct((M, D), F32)] * 3,
        scratch_shapes=[pltpu.VMEM((tm, D), BF16)],
        compiler_params=_params(("parallel", "arbitrary"), blocks, [((tm, D), BF16)]),
        name="qkv_proj",
    )(x, g, w_qkv, w_qkv, w_qkv, qg, kg, cos, sin)


def _topk_rank(g, blk_id):
    rank = jnp.zeros(g.shape, jnp.int32)
    for m in range(g.shape[0]):
        gm = g[m:m + 1]
        beats = (gm > g) | ((gm == g) & (m < blk_id))
        rank = rank + beats.astype(jnp.int32)
    return rank


Q_GROUP = 4


def _moba_prompt_kernel(q_ref, k_ref, v_ref, o_ref, kb_ref, vt_ref, km_ref, bias_ref, s_ref, acc_ref,
                        qt_ref, mx_ref, l_ref, *, scale):
    blk = MOBA_BLOCK
    seq, hd = k_ref.shape
    nb = seq // blk
    tq = q_ref.shape[0]
    g = pl.program_id(2)
    first = g * Q_GROUP

    @pl.when(g == 0)
    def _():
        kb_ref[...] = k_ref[...].astype(BF16)
        for n in range(nb):
            km_ref[n:n + 1, :] = jnp.mean(k_ref[n * blk:(n + 1) * blk, :], axis=0, keepdims=True)
            vt_ref[n] = v_ref[n * blk:(n + 1) * blk, :].T.astype(BF16)

    q = q_ref[...]
    gate = lax.dot_general(km_ref[...], q, (((1,), (1,)), ((), ())),
                           precision=lax.Precision.HIGHEST, preferred_element_type=F32)
    blk_id = lax.broadcasted_iota(jnp.int32, gate.shape, 0)
    own = first + lax.broadcasted_iota(jnp.int32, gate.shape, 1) // blk
    past = blk_id < own
    rank = _topk_rank(jnp.where(past, gate, NEG_INF), blk_id)
    bias_ref[...] = jnp.where((rank < MOBA_TOPK) & past, 0.0, NEG_INF)

    qt = (q * scale).T.astype(BF16)
    qt_ref[...] = qt

    def scores(n):
        return _dot(kb_ref[pl.ds(pl.multiple_of(n * blk, blk), blk), :], qt)

    def sublane_groups(x):
        return x.reshape(blk // 8, 8, tq)

    def past_scores(c, mx):
        for u in range(Q_GROUP):
            n = c * Q_GROUP + u
            s = scores(n) + bias_ref[pl.ds(n, 1), :]
            s_ref[n] = s
            mx = jnp.maximum(mx, jnp.max(sublane_groups(s), axis=0))
        return mx

    mx = lax.fori_loop(0, g, past_scores, jnp.full((8, tq), NEG_INF, F32))

    mx_ref[...] = mx
    for u in range(Q_GROUP):
        n, c0, w = first + u, u * blk, tq - u * blk
        s = _dot(kb_ref[pl.ds(pl.multiple_of(n * blk, blk), blk), :], qt_ref[:, c0:])
        kpos = lax.broadcasted_iota(jnp.int32, (blk, w), 0)
        qcol = lax.broadcasted_iota(jnp.int32, (blk, w), 1)
        s = jnp.where(qcol < blk, jnp.where(kpos <= qcol, s, NEG_INF), s + bias_ref[pl.ds(n, 1), c0:])
        s_ref[n, :, c0:] = s
        mx_ref[:, c0:] = jnp.maximum(mx_ref[:, c0:], jnp.max(s.reshape(blk // 8, 8, w), axis=0))
    m = jnp.max(mx_ref[...], axis=0, keepdims=True)

    def weigh(n, l8):
        p = jnp.exp(s_ref[n] - m)
        acc_ref[...] += _dot(vt_ref[n], p.astype(BF16))
        return l8 + jnp.sum(sublane_groups(p), axis=0)

    def weigh_own(u):
        n, c0, w = first + u, u * blk, tq - u * blk
        p = jnp.exp(s_ref[n, :, c0:] - jnp.max(mx_ref[:, c0:], axis=0, keepdims=True))
        acc_ref[:, c0:] += _dot(vt_ref[n], p.astype(BF16))
        l_ref[:, c0:] += jnp.sum(p.reshape(blk // 8, 8, w), axis=0)

    def past_weigh(c, l8):
        for u in range(Q_GROUP):
            l8 = weigh(c * Q_GROUP + u, l8)
        return l8

    acc_ref[...] = jnp.zeros(acc_ref.shape, F32)
    l_ref[...] = lax.fori_loop(0, g, past_weigh, jnp.zeros((8, tq), F32))
    for u in range(Q_GROUP):
        weigh_own(u)
    l = jnp.sum(l_ref[...], axis=0, keepdims=True)
    o_ref[...] = (acc_ref[...] / l).T.astype(o_ref.dtype)


def _moba_prompt(q, k, v):
    B, S, D = q.shape
    hd = D // N_HEADS
    tq = Q_GROUP * MOBA_BLOCK
    assert hd == V7X_LANES and S % tq == 0
    nb = S // MOBA_BLOCK
    tile = pl.BlockSpec((None, tq, hd), lambda b, h, g: (b, g, h))
    full = pl.BlockSpec((None, S, hd), lambda b, h, g: (b, 0, h))
    scratch = [((S, hd), BF16), ((nb, hd, MOBA_BLOCK), BF16), ((nb, hd), F32), ((nb, tq), F32),
               ((nb, MOBA_BLOCK, tq), F32), ((hd, tq), F32), ((hd, tq), BF16), ((8, tq), F32), ((8, tq), F32)]
    return pl.pallas_call(
        functools.partial(_moba_prompt_kernel, scale=hd ** -0.5),
        grid=(B, N_HEADS, S // tq),
        in_specs=[tile, full, full],
        out_specs=tile,
        out_shape=jax.ShapeDtypeStruct((B, S, D), BF16),
        scratch_shapes=[pltpu.VMEM(s, d) for s, d in scratch],
        compiler_params=_params(("parallel", "parallel", "arbitrary"),
                                [((tq, hd), F32), ((S, hd), F32), ((S, hd), F32), ((tq, hd), BF16)], scratch),
        name="moba_prompt",
    )(q, k, v)


def _oproj_kernel(o_ref, w_ref, x_ref, y_ref):
    y_ref[...] = x_ref[...] + _dot(o_ref[...], w_ref[...])


def _oproj(o, w, layer, x, *, tm):
    M, D = x.shape
    row = pl.BlockSpec((tm, D), lambda m: (m, 0))
    return pl.pallas_call(
        _oproj_kernel,
        grid=(M // tm,),
        in_specs=[row, pl.BlockSpec((None, D, D), lambda m: (layer, 0, 0)), row],
        out_specs=row,
        out_shape=jax.ShapeDtypeStruct((M, D), F32),
        compiler_params=_params(("parallel",), [((tm, D), BF16), ((D, D), BF16), ((tm, D), F32), ((tm, D), F32)]),
        name="out_proj",
    )(o, w, x)


def _ffn_kernel(x_ref, g_ref, wg_ref, wu_ref, wd_ref, y_ref, xn_ref):
    @pl.when(pl.program_id(1) == 0)
    def _():
        x = x_ref[...]
        xn_ref[...] = _rms(x, g_ref[...]).astype(BF16)
        y_ref[...] = x

    xn = xn_ref[...]
    gate = _dot(xn, wg_ref[...])
    up = _dot(xn, wu_ref[...])
    h = (gate / (1.0 + jnp.exp(-gate)) * up).astype(BF16)
    y_ref[...] += _dot(h, wd_ref[...])


def _ffn(x, g, w_gu, w_dn, layer, *, tm, tf):
    M, D = x.shape
    d_ff = w_dn.shape[1]
    nf = d_ff // tf
    assert nf * tf == d_ff
    row = pl.BlockSpec((tm, D), lambda m, f: (m, 0))
    blocks = [((tm, D), F32), ((D, tf), BF16), ((D, tf), BF16), ((tf, D), BF16), ((tm, D), F32)]
    return pl.pallas_call(
        _ffn_kernel,
        grid=(M // tm, nf),
        in_specs=[row, pl.BlockSpec((1, D), lambda m, f: (0, 0)),
                  pl.BlockSpec((None, D, tf), lambda m, f: (layer, 0, f)),
                  pl.BlockSpec((None, D, tf), lambda m, f: (layer, 0, f + nf)),
                  pl.BlockSpec((None, tf, D), lambda m, f: (layer, f, 0))],
        out_specs=row,
        out_shape=jax.ShapeDtypeStruct((M, D), F32),
        scratch_shapes=[pltpu.VMEM((tm, D), BF16)],
        compiler_params=_params(("parallel", "arbitrary"), blocks, [((tm, D), BF16)]),
        name="swiglu_ffn",
    )(x, g, w_gu, w_gu, w_dn)


def _pool_prompt_kernel(x_ref, g_ref, w_ref, sc_ref, y_ref, tail_ref, ext_ref):
    tm, D = x_ref.shape
    carry = POOL_CARRY_ROWS
    gd = D // len(POOL_WINDOWS)
    si = pl.program_id(1)

    @pl.when(si == 0)
    def _():
        ext_ref[0:carry, :] = jnp.zeros((carry, D), F32)

    x = x_ref[...]
    h = _rms(x, g_ref[...])
    ext_ref[carry:carry + tm, :] = h
    pos = si * tm + lax.broadcasted_iota(jnp.int32, (tm, 1), 0)
    for gi, w in enumerate(POOL_WINDOWS):
        lo, hi = gi * gd, (gi + 1) * gd
        hg = h[:, lo:hi]
        win = hg
        for back in range(1, w):
            win = win + ext_ref[carry - back:carry - back + tm, lo:hi]
        cnt = jnp.minimum(pos + 1, w).astype(F32)
        d = (win / cnt - hg).astype(BF16)
        y_ref[:, lo:hi] = x[:, lo:hi] + _dot(d, w_ref[gi]) * sc_ref[:, lo:hi]
    last = h[tm - carry:tm, :]
    ext_ref[0:carry, :] = last
    tail_ref[...] = last


def _pool_prompt(x, g, w_pool, layer, sc, *, tm):
    B, S, D = x.shape
    gd = D // len(POOL_WINDOWS)
    assert S % tm == 0 and tm >= POOL_CARRY_ROWS
    row = pl.BlockSpec((None, tm, D), lambda b, s: (b, s, 0))
    vec = pl.BlockSpec((1, D), lambda b, s: (0, 0))
    return pl.pallas_call(
        _pool_prompt_kernel,
        grid=(B, S // tm),
        in_specs=[row, vec, pl.BlockSpec((None, len(POOL_WINDOWS), gd, gd), lambda b, s: (layer, 0, 0, 0)), vec],
        out_specs=[row, pl.BlockSpec((None, POOL_CARRY_ROWS, D), lambda b, s: (b, 0, 0))],
        out_shape=[jax.ShapeDtypeStruct((B, S, D), F32), jax.ShapeDtypeStruct((B, POOL_CARRY_ROWS, D), F32)],
        scratch_shapes=[pltpu.VMEM((POOL_CARRY_ROWS + tm, D), F32)],
        compiler_params=_params(("parallel", "arbitrary"),
                                [((tm, D), F32), ((len(POOL_WINDOWS), gd, gd), BF16), ((tm, D), F32)],
                                [((POOL_CARRY_ROWS + tm, D), F32)]),
        name="pool_prompt",
    )(x, g, w_pool, sc)


def _pool_sample_kernel(x_ref, pre_ref, g_ref, w_ref, sc_ref, y_ref, h_ref, *, counts):
    D = x_ref.shape[1]
    gd = D // len(POOL_WINDOWS)
    x = x_ref[...]
    h = _rms(x, g_ref[...])
    h_ref[...] = h
    n_pre = pre_ref.shape[1]
    for gi, w in enumerate(POOL_WINDOWS):
        lo, hi = gi * gd, (gi + 1) * gd
        hg = h[:, lo:hi]
        win = hg
        for back in range(1, w):
            win = win + pre_ref[:, n_pre - back, lo:hi]
        d = (win / counts[gi] - hg).astype(BF16)
        y_ref[:, lo:hi] = x[:, lo:hi] + _dot(d, w_ref[gi]) * sc_ref[:, lo:hi]


def _pool_sample(x, prefix, g, w_pool, sc, *, start_pos):
    B, D = x.shape
    counts = tuple(float(min(start_pos + 1, w)) for w in POOL_WINDOWS)
    return pl.pallas_call(
        functools.partial(_pool_sample_kernel, counts=counts),
        out_shape=[jax.ShapeDtypeStruct((B, D), F32), jax.ShapeDtypeStruct((B, D), F32)],
        name="pool_sample",
    )(x, prefix, g, w_pool, sc)


def _block_ksum_kernel(pt_ref, *refs, pages_per_block):
    page_refs, o_ref = refs[:-1], refs[-1]
    for i in range(o_ref.shape[0]):
        total = jnp.sum(page_refs[i * pages_per_block][...], axis=0)
        for r in page_refs[i * pages_per_block + 1:(i + 1) * pages_per_block]:
            total = total + jnp.sum(r[...], axis=0)
        o_ref[i] = total


KSUM_BLOCKS_PER_STEP = 2


def _block_ksum(cache_k, page_table, layer, *, pages_per_block):
    _, _, page, H, hd = cache_k.shape
    B, n_pages = page_table.shape
    n_blocks = n_pages // pages_per_block
    pages_per_step = KSUM_BLOCKS_PER_STEP * pages_per_block
    assert n_blocks % KSUM_BLOCKS_PER_STEP == 0

    def page_spec(i):
        return pl.BlockSpec((None, None, page, H, hd),
                            lambda b, n, pt: (layer, pt[b, n * pages_per_step + i], 0, 0, 0))

    return pl.pallas_call(
        functools.partial(_block_ksum_kernel, pages_per_block=pages_per_block),
        grid_spec=pltpu.PrefetchScalarGridSpec(
            num_scalar_prefetch=1,
            grid=(B, n_blocks // KSUM_BLOCKS_PER_STEP),
            in_specs=[page_spec(i) for i in range(pages_per_step)],
            out_specs=pl.BlockSpec((None, KSUM_BLOCKS_PER_STEP, H, hd), lambda b, n, pt: (b, n, 0, 0)),
        ),
        out_shape=jax.ShapeDtypeStruct((B, n_blocks, H, hd), F32),
        compiler_params=_params(("parallel", "parallel"),
                                [((page, H, hd), F32)] * pages_per_step + [((KSUM_BLOCKS_PER_STEP, H, hd), F32)]),
        name="block_ksum",
    )(page_table, *([cache_k] * pages_per_step))


def _sample_select_kernel(ks_ref, q_ref, sel_ref):
    k_mean = ks_ref[...] * (1.0 / MOBA_BLOCK)
    gate = jnp.sum(k_mean * q_ref[...][None, :, :], axis=-1, keepdims=True)
    blk_id = lax.broadcasted_iota(jnp.int32, gate.shape, 0)
    rank = _topk_rank(gate, blk_id)
    for r in range(MOBA_TOPK):
        picked = jnp.sum(jnp.where(rank == r, blk_id, 0), axis=0)
        sel_ref[r] = jnp.broadcast_to(picked, sel_ref.shape[1:])


def _sample_select(ksum, q):
    B, n_blocks, H, hd = ksum.shape
    assert n_blocks >= MOBA_TOPK
    sel = pl.pallas_call(
        _sample_select_kernel,
        grid=(B,),
        in_specs=[pl.BlockSpec((None, n_blocks, H, hd), lambda b: (b, 0, 0, 0)),
                  pl.BlockSpec((None, H, hd), lambda b: (b, 0, 0))],
        out_specs=pl.BlockSpec((None, MOBA_TOPK, H, hd), lambda b: (b, 0, 0, 0)),
        out_shape=jax.ShapeDtypeStruct((B, MOBA_TOPK, H, hd), jnp.int32),
        compiler_params=_params(("parallel",), [((n_blocks, H, hd), F32), ((H, hd), F32), ((MOBA_TOPK, H, hd), F32)]),
        name="sample_select",
    )(ksum, q)
    return sel[:, :, :, 0]


def _sample_attn_kernel(pages_ref, q_ref, kn_ref, vn_ref, ck_hbm, cv_hbm, o_ref, kbuf, vbuf, ksem, vsem,
                        *, scale, layer, n_sel_pages):
    b = pl.program_id(0)
    n_heads = q_ref.shape[0]
    page_size = kbuf.shape[1] // n_sel_pages

    def copies(h, t):
        page = pages_ref[(b * n_heads + h) * n_sel_pages + t]
        rows = pl.ds(t * page_size, page_size)
        slot = h * n_sel_pages + t
        return (pltpu.make_async_copy(ck_hbm.at[layer, page, :, h, :], kbuf.at[h, rows], ksem.at[slot]),
                pltpu.make_async_copy(cv_hbm.at[layer, page, :, h, :], vbuf.at[h, rows], vsem.at[slot]))

    def start_head(h, carry):
        for t in range(n_sel_pages):
            for c in copies(h, t):
                c.start()
        return carry

    lax.fori_loop(0, n_heads, start_head, 0)

    def head(h, carry):
        for t in range(n_sel_pages):
            for c in copies(h, t):
                c.wait()
        row = pl.ds(h, 1)
        q = q_ref[row, :] * scale
        s_new = jnp.sum(q * kn_ref[row, :], axis=-1, keepdims=True)
        s = jnp.sum(kbuf[h] * q, axis=-1, keepdims=True)
        m = jnp.maximum(s_new, jnp.max(s, axis=0, keepdims=True))
        p_new = jnp.exp(s_new - m)
        p = jnp.exp(s - m)
        l = p_new + jnp.sum(p, axis=0, keepdims=True)
        acc = p_new * vn_ref[row, :] + jnp.sum(p * vbuf[h], axis=0, keepdims=True)
        o_ref[row, :] = acc / l
        return carry

    lax.fori_loop(0, n_heads, head, 0)


def _sample_attn(pages, q, k_new, v_new, cache_k, cache_v, layer, *, n_sel_pages):
    B, H, hd = q.shape
    page = cache_k.shape[2]
    vec = pl.BlockSpec((None, H, hd), lambda b, pg: (b, 0, 0))
    hbm = pl.BlockSpec(memory_space=pl.ANY)
    bufs = [((H, n_sel_pages * page, hd), F32)] * 2
    return pl.pallas_call(
        functools.partial(_sample_attn_kernel, scale=hd ** -0.5, layer=layer, n_sel_pages=n_sel_pages),
        grid_spec=pltpu.PrefetchScalarGridSpec(
            num_scalar_prefetch=1,
            grid=(B,),
            in_specs=[vec, vec, vec, hbm, hbm],
            out_specs=vec,
            scratch_shapes=[pltpu.VMEM(s_, d_) for s_, d_ in bufs] + [pltpu.SemaphoreType.DMA((H * n_sel_pages,))] * 2,
        ),
        out_shape=jax.ShapeDtypeStruct((B, H, hd), F32),
        compiler_params=_params(("arbitrary",), [((H, hd), F32)] * 4, bufs),
        name="sample_attn",
    )(pages, q, k_new, v_new, cache_k, cache_v)


def _rope_tables(pos, head_dim):
    half = head_dim // 2
    inv = ROPE_THETA ** (-jnp.arange(half, dtype=F32) / half)
    ang = pos.astype(F32)[:, None] * inv[None, :]
    cos, sin = jnp.cos(ang), jnp.sin(ang)
    return jnp.concatenate([cos, cos], axis=1), jnp.concatenate([-sin, sin], axis=1)


def kernel(x_prompt, x_sample, cache_k, cache_v, state_pool, page_table, norm_mix, norm_ffn, w_qkv, w_o, q_norm, k_norm, w_pool, pool_scale, w_gate_up, w_down):
    Bp, S, D = x_prompt.shape
    Bd, L, _ = x_sample.shape
    depth = norm_mix.shape[0]
    n_layers_attn, n_phys, page_size, n_heads, hd = cache_k.shape
    n_pages = page_table.shape[1]
    past = n_pages * page_size
    assert n_heads == N_HEADS and L == 1 and MOBA_BLOCK % page_size == 0 and past % MOBA_BLOCK == 0
    pages_per_block = MOBA_BLOCK // page_size
    n_sel_pages = MOBA_TOPK * pages_per_block

    w_qkv_b, w_o_b = w_qkv.astype(BF16), w_o.astype(BF16)
    w_gu_b, w_dn_b, w_pool_b = w_gate_up.astype(BF16), w_down.astype(BF16), w_pool.astype(BF16)
    cos_p, sin_p = _rope_tables(jnp.tile(jnp.arange(S), Bp), hd)
    cos_s, sin_s = _rope_tables(jnp.tile(past + jnp.arange(L), Bd), hd)

    Mp, Ms = Bp * S, Bd * L
    xp = x_prompt.reshape(Mp, D)
    xs = x_sample.reshape(Ms, D)
    tm_p, tm_s = 512, Ms
    kp_list, vp_list, ks_list, vs_list, sp_list, ss_list = [], [], [], [], [], []
    ksums = [_block_ksum(cache_k, page_table, a, pages_per_block=pages_per_block) for a in range(n_layers_attn)]
    for i in range(depth):
        g_mix = norm_mix[i][None, :]
        g_ffn = norm_ffn[i][None, :]
        if i % 2 == 0:
            a = i // 2
            qg, kg = q_norm[a][None, :], k_norm[a][None, :]
            q, k, v = _qkv(xp, g_mix, w_qkv_b, a, qg, kg, cos_p, sin_p, tm=tm_p, tn=512)
            o = _moba_prompt(q.reshape(Bp, S, D), k.reshape(Bp, S, D), v.reshape(Bp, S, D))
            xp = _oproj(o.reshape(Mp, D), w_o_b, a, xp, tm=tm_p)
            kp_list.append(k.reshape(Bp, S, N_HEADS, hd))
            vp_list.append(v.reshape(Bp, S, N_HEADS, hd))
            qs, kn, vn = _qkv(xs, g_mix, w_qkv_b, a, qg, kg, cos_s, sin_s, tm=tm_s, tn=512)
            sel = _sample_select(ksums[a], qs.reshape(Bd, N_HEADS, hd))
            logical = sel[:, :, :, None] * pages_per_block + jnp.arange(pages_per_block)
            logical = logical.transpose(0, 2, 1, 3).reshape(Bd, N_HEADS * n_sel_pages)
            pages = jnp.take_along_axis(page_table, logical, axis=1).reshape(-1)
            os_ = _sample_attn(pages, qs.reshape(Bd, N_HEADS, hd), kn.reshape(Bd, N_HEADS, hd),
                               vn.reshape(Bd, N_HEADS, hd), cache_k, cache_v, a, n_sel_pages=n_sel_pages)
            xs = _oproj(os_.reshape(Ms, D).astype(BF16), w_o_b, a, xs, tm=tm_s)
            ks_list.append(kn.reshape(Bd, L, N_HEADS, hd))
            vs_list.append(vn.reshape(Bd, L, N_HEADS, hd))
        else:
            p = i // 2
            sc = pool_scale[p][None, :]
            y, tail = _pool_prompt(xp.reshape(Bp, S, D), g_mix, w_pool_b, p, sc, tm=tm_p)
            xp = y.reshape(Mp, D)
            sp_list.append(tail[:, POOL_CARRY_ROWS - POOL_STATE_LEN:, :])
            xs, hs = _pool_sample(xs, state_pool[p], g_mix, w_pool_b[p], sc, start_pos=past)
            ss_list.append(jnp.concatenate([state_pool[p][:, 1:, :], hs[:, None, :]], axis=1))
        xp = _ffn(xp, g_ffn, w_gu_b, w_dn_b, i, tm=tm_p, tf=512)
        xs = _ffn(xs, g_ffn, w_gu_b, w_dn_b, i, tm=tm_s, tf=512)
    return (xp.reshape(Bp, S, D), xs.reshape(Bd, L, D), jnp.stack(kp_list), jnp.stack(vp_list),
            jnp.stack(ks_list), jnp.stack(vs_list), jnp.stack(sp_list), jnp.stack(ss_list))
```
